```python
import math
import jax, jax.numpy as jnp
from jax import lax
import numpy as np

D_MODEL = 4096
BATCH = 2
SEQ = 4096
DEPTH = 2

HEAD_DIM = 128
BLOCK = 128
SCALE = HEAD_DIM ** -0.5
A_HEADS = 6
A_V_DIM = 2 * HEAD_DIM
B_Q_HEADS = 8
B_KV_HEADS = 2
B_WINDOW = 128
C_GROUPS = ((128, 1), (512, 4), (2048, 16))
C_HEADS_PER_GROUP = 4
C_HEADS = C_HEADS_PER_GROUP * len(C_GROUPS)
N_ALIBI_HEADS = B_Q_HEADS + C_HEADS + A_HEADS
A_WIDTH = A_HEADS * A_V_DIM
B_WIDTH = B_Q_HEADS * HEAD_DIM
C_WIDTH = C_HEADS * HEAD_DIM
MIX_WIDTH = A_WIDTH + B_WIDTH + C_WIDTH
N_BRANCHES = 3
IN_SIZES = (A_HEADS * 2 * HEAD_DIM, A_HEADS * 2 * HEAD_DIM, A_WIDTH,
            B_Q_HEADS * HEAD_DIM, B_KV_HEADS * HEAD_DIM, B_KV_HEADS * HEAD_DIM,
            C_WIDTH, C_WIDTH, C_WIDTH,
            N_BRANCHES * D_MODEL)
V_BLOCKS = (2, 5, 8)
N_IN = sum(IN_SIZES)
D_FF = 11008
CONV_WIDTH = 3
LN_EPS = 1e-5
RMS_EPS = 1e-5
ALPHA = (2 * DEPTH) ** 0.25
BETA = (8 * DEPTH) ** -0.25

kernel_name = "hybrid_gated_diff_swa_dilated_deepnorm"


def layer_norm(x, g, b):
    x32 = x.astype(jnp.float32)
    mu = jnp.mean(x32, -1, keepdims=True)
    var = jnp.mean(jnp.square(x32 - mu), -1, keepdims=True)
    y = (x32 - mu) * lax.rsqrt(var + LN_EPS) * g.astype(jnp.float32) + b.astype(jnp.float32)
    return y.astype(x.dtype)


def alibi_slopes(n):
    return jnp.exp2(-8.0 * (jnp.arange(n, dtype=jnp.float32) + 1.0) / n)


def diff_attention(q, k, v, lam_params, norm_g, slopes, lam_init):
    b, s, _ = q.shape
    nb = s // BLOCK
    qb = q.reshape(b, nb, BLOCK, A_HEADS, 2, HEAD_DIM).transpose(1, 0, 3, 4, 2, 5)
    kk = k.reshape(b, s, A_HEADS, 2, HEAD_DIM).transpose(0, 2, 3, 1, 4)
    vv = v.reshape(b, s, A_HEADS, A_V_DIM).transpose(0, 2, 1, 3)
    lp = lam_params.astype(jnp.float32)
    lam = jnp.exp(jnp.dot(lp[0], lp[1])) - jnp.exp(jnp.dot(lp[2], lp[3])) + lam_init
    key_pos = jnp.arange(s)

    def block(args):
        qblk, i = args
        dist = (i * BLOCK + jnp.arange(BLOCK))[:, None] - key_pos[None, :]
        bias = -slopes[:, None, None] * dist.astype(jnp.float32)
        sc = jnp.einsum('bhmqd,bhmkd->bhmqk', qblk, kk).astype(jnp.float32) * SCALE + bias[None, :, None]
        sc = jnp.where(dist >= 0, sc, -jnp.inf)
        p = jax.nn.softmax(sc, axis=-1)
        attn = p[:, :, 0] - lam * p[:, :, 1]
        return jnp.einsum('bhqk,bhkv->bhqv', attn.astype(vv.dtype), vv)

    o = lax.map(block, (qb, jnp.arange(nb)))
    o = o.transpose(1, 0, 3, 2, 4).reshape(b, s, A_HEADS, A_V_DIM)
    o32 = o.astype(jnp.float32)
    o32 = o32 * lax.rsqrt(jnp.mean(jnp.square(o32), -1, keepdims=True) + RMS_EPS) * norm_g.astype(jnp.float32)
    return (o32 * (1.0 - lam_init)).astype(q.dtype).reshape(b, s, A_WIDTH)


def banded_attention(q, k, v, slopes, step, max_dist, sinks, return_lse):
    b, n, hk, g, dh = q.shape
    nb = n // BLOCK
    qb = q.reshape(b, nb, BLOCK, hk, g, dh)

    def two_blocks(t):
        tp = jnp.pad(t, ((0, 0), (BLOCK, 0), (0, 0), (0, 0))).reshape(b, nb + 1, BLOCK, hk, dh)
        return jnp.concatenate([tp[:, :-1], tp[:, 1:]], axis=2)

    kb, vb = two_blocks(k), two_blocks(v)
    dist = jnp.arange(BLOCK)[:, None] + BLOCK - jnp.arange(2 * BLOCK)[None, :]
    key_idx = (jnp.arange(nb) * BLOCK - BLOCK)[:, None, None] + jnp.arange(2 * BLOCK)[None, None, :]
    valid = (dist >= 0) & (dist <= max_dist) & (key_idx >= 0)
    bias = -slopes.astype(jnp.float32)[:, :, None, None] * (dist * step).astype(jnp.float32)
    sc = jnp.einsum('bnqhgd,bnkhd->bnhgqk', qb, kb).astype(jnp.float32) * SCALE + bias
    sc = jnp.where(valid[None, :, None, None], sc, -jnp.inf)
    m = jnp.max(sc, -1)
    if sinks is not None:
        sk = sinks.astype(jnp.float32)[:, :, None]
        m = jnp.maximum(m, sk)
    e = jnp.exp(sc - m[..., None])
    denom = jnp.sum(e, -1)
    if sinks is not None:
        denom = denom + jnp.exp(sk - m)
    p = e / denom[..., None]
    o = jnp.einsum('bnhgqk,bnkhd->bnqhgd', p.astype(vb.dtype), vb).reshape(b, n, hk, g, dh)
    if return_lse:
        lse = (m + jnp.log(denom)).transpose(0, 1, 4, 2, 3).reshape(b, n, hk, g)
        return o, lse
    return o


def sliding_window_sink_attention(q, k, v, sinks, slopes):
    b, s, _ = q.shape
    grp = B_Q_HEADS // B_KV_HEADS
    o = banded_attention(q.reshape(b, s, B_KV_HEADS, grp, HEAD_DIM),
                         k.reshape(b, s, B_KV_HEADS, HEAD_DIM),
                         v.reshape(b, s, B_KV_HEADS, HEAD_DIM),
                         slopes.reshape(B_KV_HEADS, grp), 1, B_WINDOW - 1,
                         sinks.reshape(B_KV_HEADS, grp), False)
    return o.astype(q.dtype).reshape(b, s, B_WIDTH)


def dilated_attention(q, k, v, slopes):
    b, s, _ = q.shape
    hpg = C_HEADS_PER_GROUP
    qh = q.reshape(b, s, C_HEADS, HEAD_DIM)
    kh = k.reshape(b, s, C_HEADS, HEAD_DIM)
    vh = v.reshape(b, s, C_HEADS, HEAD_DIM)
    outs, lses = [], []
    for gi, (window, dil) in enumerate(C_GROUPS):
        lo, hi = gi * hpg, (gi + 1) * hpg
        n = s // dil
        n_pad = -(-n // BLOCK) * BLOCK

        def to_sub(t):
            t = t[:, :, lo:hi].reshape(b, n, dil, hpg, HEAD_DIM).transpose(0, 2, 1, 3, 4)
            t = t.reshape(b * dil, n, hpg, HEAD_DIM)
            return jnp.pad(t, ((0, 0), (0, n_pad - n), (0, 0), (0, 0)))

        o, lse = banded_attention(to_sub(qh)[:, :, :, None], to_sub(kh), to_sub(vh),
                                  slopes[lo:hi][:, None], dil, window // dil, None, True)
        o = o[:, :n, :, 0].reshape(b, dil, n, hpg, HEAD_DIM).transpose(0, 2, 1, 3, 4).reshape(b, s, hpg, HEAD_DIM)
        lse = lse[:, :n, :, 0].reshape(b, dil, n, hpg).transpose(0, 2, 1, 3).reshape(b, s, hpg)
        outs.append(o)
        lses.append(lse)
    w = jax.nn.softmax(jnp.stack(lses, axis=0), axis=0)
    o = jnp.concatenate([(w[gi][..., None] * outs[gi]).astype(q.dtype) for gi in range(len(C_GROUPS))], axis=2)
    return o.reshape(b, s, C_WIDTH)


def conv_glu_mlp(x, w_up, conv_w, conv_b, w_down):
    s = x.shape[1]
    h = x @ w_up
    hp = jnp.pad(h, ((0, 0), (CONV_WIDTH - 1, 0), (0, 0)))
    c = conv_b + conv_w[0] * hp[:, 0:s]
    for j in range(1, CONV_WIDTH):
        c = c + conv_w[j] * hp[:, j:j + s]
    gate, val = jnp.split(c, 2, axis=-1)
    return (jax.nn.silu(gate) * val) @ w_down


def setup_inputs(seed: int = 0) -> dict:
    key = jax.random.key(seed)
    ks = jax.random.split(key, 16)

    def nrm(k, shape, scale):
        return jax.random.normal(k, shape, jnp.float32) * scale

    offs = np.cumsum((0,) + IN_SIZES)
    col_scale = np.ones((N_IN,), np.float32)
    for i in V_BLOCKS:
        col_scale[offs[i]:offs[i + 1]] = BETA
    return {
        "x": nrm(ks[0], (BATCH, SEQ, D_MODEL), 1.0),
        "w_in": nrm(ks[1], (DEPTH, D_MODEL, N_IN), D_MODEL ** -0.5) * jnp.asarray(col_scale),
        "diff_lambda": nrm(ks[2], (DEPTH, 4, HEAD_DIM), 0.1),
        "diff_norm_g": 1.0 + nrm(ks[3], (DEPTH, A_V_DIM), 0.02),
        "sink_logits": nrm(ks[4], (DEPTH, B_Q_HEADS), 0.5),
        "w_branch": nrm(ks[5], (DEPTH, MIX_WIDTH, D_MODEL), BETA * MIX_WIDTH ** -0.5),
        "w_o": nrm(ks[6], (DEPTH, D_MODEL, D_MODEL), BETA * D_MODEL ** -0.5),
        "ln1_g": 1.0 + nrm(ks[7], (DEPTH, D_MODEL), 0.02),
        "ln1_b": nrm(ks[8], (DEPTH, D_MODEL), 0.02),
        "w_up": nrm(ks[9], (DEPTH, D_MODEL, 2 * D_FF), BETA * D_MODEL ** -0.5),
        "conv_w": nrm(ks[10], (DEPTH, CONV_WIDTH, 2 * D_FF), CONV_WIDTH ** -0.5),
        "conv_b": nrm(ks[11], (DEPTH, 2 * D_FF), 0.02),
        "w_down": nrm(ks[12], (DEPTH, D_FF, D_MODEL), BETA * D_FF ** -0.5),
        "ln2_g": 1.0 + nrm(ks[13], (DEPTH, D_MODEL), 0.02),
        "ln2_b": nrm(ks[14], (DEPTH, D_MODEL), 0.02),
    }


def reference(x, w_in, diff_lambda, diff_norm_g, sink_logits, w_branch, w_o, ln1_g, ln1_b,
              w_up, conv_w, conv_b, w_down, ln2_g, ln2_b):
    slopes = alibi_slopes(N_ALIBI_HEADS)
    sl_b = slopes[:B_Q_HEADS]
    sl_c = slopes[B_Q_HEADS:B_Q_HEADS + C_HEADS]
    sl_a = slopes[B_Q_HEADS + C_HEADS:]
    split_pts = np.cumsum(IN_SIZES)[:-1].tolist()
    for l in range(DEPTH):
        lam_init = 0.8 - 0.6 * math.exp(-0.3 * l)
        proj = x @ w_in[l]
        aq, ak, av, bq, bk, bv, cq, ck, cv, gates = jnp.split(proj, split_pts, axis=-1)
        o_a = diff_attention(aq, ak, av, diff_lambda[l], diff_norm_g[l], sl_a, lam_init)
        o_b = sliding_window_sink_attention(bq, bk, bv, sink_logits[l], sl_b)
        o_c = dilated_attention(cq, ck, cv, sl_c)
        g = jax.nn.sigmoid(gates)
        wb = w_branch[l]
        y = (g[..., :D_MODEL] * (o_a @ wb[:A_WIDTH])
             + g[..., D_MODEL:2 * D_MODEL] * (o_b @ wb[A_WIDTH:A_WIDTH + B_WIDTH])
             + g[..., 2 * D_MODEL:] * (o_c @ wb[A_WIDTH + B_WIDTH:]))
        x = layer_norm(ALPHA * x + y @ w_o[l], ln1_g[l], ln1_b[l])
        x = layer_norm(ALPHA * x + conv_glu_mlp(x, w_up[l], conv_w[l], conv_b[l], w_down[l]), ln2_g[l], ln2_b[l])
    return x
```

```python
import functools
import math

import numpy as np
import jax
import jax.numpy as jnp
from jax import lax
from jax.experimental import pallas as pl
from jax.experimental.pallas import tpu as pltpu

F32 = jnp.float32
BF16 = jnp.bfloat16

HEAD_DIM = 128
BLOCK = 128
SCALE = HEAD_DIM ** -0.5
A_HEADS = 6
A_V_DIM = 2 * HEAD_DIM
B_Q_HEADS = 8
B_KV_HEADS = 2
B_WINDOW = 128
C_GROUPS = ((128, 1), (512, 4), (2048, 16))
C_HEADS_PER_GROUP = 4
C_HEADS = C_HEADS_PER_GROUP * len(C_GROUPS)
N_ALIBI_HEADS = B_Q_HEADS + C_HEADS + A_HEADS
A_WIDTH = A_HEADS * A_V_DIM
B_WIDTH = B_Q_HEADS * HEAD_DIM
C_WIDTH = C_HEADS * HEAD_DIM
CONV_WIDTH = 3
LN_EPS = 1e-5
RMS_EPS = 1e-5
LOG2E = math.log2(math.e)
LN2 = math.log(2.0)
NEG_BIG = -1e30

OFF_AQ = 0
OFF_AK = OFF_AQ + A_WIDTH
OFF_AV = OFF_AK + A_WIDTH
OFF_BQ = OFF_AV + A_WIDTH
OFF_BK = OFF_BQ + B_WIDTH
OFF_BV = OFF_BK + B_KV_HEADS * HEAD_DIM
OFF_CQ = OFF_BV + B_KV_HEADS * HEAD_DIM
OFF_CK = OFF_CQ + C_WIDTH
OFF_CV = OFF_CK + C_WIDTH
QKV_WIDTH = OFF_CV + C_WIDTH

V7X_VMEM_LIMIT_BYTES = 60 * 1024 * 1024
ROW_CHUNK = 256
CAST_ROWS = 256


def _params(n_axes, vmem=V7X_VMEM_LIMIT_BYTES):
    return pltpu.CompilerParams(dimension_semantics=("arbitrary",) * n_axes, vmem_limit_bytes=vmem)


def _cast_weight(w_ref, wb_ref, col0=0):
    k, w = w_ref.shape

    def body(i, c):
        r = pl.multiple_of(i * CAST_ROWS, CAST_ROWS)
        wb_ref[pl.ds(r, CAST_ROWS), col0:col0 + w] = w_ref[pl.ds(r, CAST_ROWS), :].astype(BF16)
        return c

    lax.fori_loop(0, k // CAST_ROWS, body, 0)


def _cast_kernel(x_ref, o_ref):
    o_ref[...] = x_ref[...].astype(o_ref.dtype)


def cast_bf16(x, rows, layer=None):
    r, c = x.shape[-2:]
    if layer is None:
        in_spec = pl.BlockSpec((rows, c), lambda i: (i, 0))
    else:
        in_spec = pl.BlockSpec((None, rows, c), lambda i: (layer, i, 0))
    return pl.pallas_call(
        _cast_kernel,
        grid=(r // rows,),
        in_specs=[in_spec],
        out_specs=pl.BlockSpec((rows, c), lambda i: (i, 0)),
        out_shape=jax.ShapeDtypeStruct((r, c), BF16),
        compiler_params=_params(1),
        name="cast_bf16",
    )(x)


def _inproj_scale_kernel(x_ref, w_ref, cs_ref, o_ref, wb_ref):
    @pl.when(pl.program_id(1) == 0)
    def _():
        _cast_weight(w_ref, wb_ref)

    for c in range(x_ref.shape[0] // ROW_CHUNK):
        rows = slice(c * ROW_CHUNK, (c + 1) * ROW_CHUNK)
        acc = jnp.dot(x_ref[rows, :], wb_ref[...], preferred_element_type=F32)
        o_ref[rows, :] = (acc * cs_ref[...]).astype(o_ref.dtype)


def _inproj_sigmoid_kernel(x_ref, w_ref, o_ref, wb_ref):
    @pl.when(pl.program_id(1) == 0)
    def _():
        _cast_weight(w_ref, wb_ref)

    for c in range(x_ref.shape[0] // ROW_CHUNK):
        rows = slice(c * ROW_CHUNK, (c + 1) * ROW_CHUNK)
        acc = jnp.dot(x_ref[rows, :], wb_ref[...], preferred_element_type=F32)
        o_ref[rows, :] = jax.nn.sigmoid(acc).astype(o_ref.dtype)


def in_projection(xb, w, layer, col0, width, *, colscale, tm, tn):
    m, k = xb.shape
    assert col0 % tn == 0 and width % tn == 0 and m % tm == 0
    cb = col0 // tn
    in_specs = [
        pl.BlockSpec((tm, k), lambda n, i: (i, 0)),
        pl.BlockSpec((None, k, tn), lambda n, i: (layer, 0, cb + n)),
    ]
    args = [xb, w]
    if colscale is not None:
        in_specs.append(pl.BlockSpec((1, tn), lambda n, i: (0, n)))
        args.append(colscale)
    return pl.pallas_call(
        _inproj_sigmoid_kernel if colscale is None else _inproj_scale_kernel,
        grid=(width // tn, m // tm),
        in_specs=in_specs,
        out_specs=pl.BlockSpec((tm, tn), lambda n, i: (i, n)),
        out_shape=jax.ShapeDtypeStruct((m, width), BF16),
        scratch_shapes=[pltpu.VMEM((k, tn), BF16)],
        compiler_params=_params(2),
        name="inproj_gates" if colscale is None else "inproj_qkv",
    )(*args)


def _attn_a_kernel(slopes_ref, q_ref, k_ref, v_ref, lp_ref, g_ref, o_ref, *, t, lam_init, slope0):
    h = pl.program_id(1)
    qi = pl.program_id(2)
    slope = slopes_ref[slope0 + h]
    q = q_ref[...]
    col = lax.broadcasted_iota(jnp.int32, (1, t), 1)
    nt = (((1,), (1,)), ((), ()))

    def scores(mi, js):
        kj = k_ref[pl.ds(js, t), mi * HEAD_DIM:(mi + 1) * HEAD_DIM]
        return lax.dot_general(q[:, mi * HEAD_DIM:(mi + 1) * HEAD_DIM], kj, nt, preferred_element_type=F32)

    def update(carry, s, vj):
        mo, lo, acc = carry
        mn = jnp.maximum(mo, jnp.max(s, axis=-1, keepdims=True))
        a = jnp.exp2(mo - mn)
        p = jnp.exp2(s - mn)
        ln = a * lo + jnp.sum(p, axis=-1, keepdims=True)
        accn = a * acc + jnp.dot(p.astype(BF16), vj, preferred_element_type=F32)
        return mn, ln, accn

    def body(j, carry):
        js = pl.multiple_of(j * t, t)
        vj = v_ref[pl.ds(js, t), :]
        cb = slope * (col + (js - qi * t)).astype(F32)
        c0 = update(carry[0], scores(0, js) + cb, vj)
        c1 = update(carry[1], scores(1, js) + cb, vj)
        return (c0, c1)

    def init():
        return (jnp.full((t, 1), NEG_BIG, F32), jnp.zeros((t, 1), F32), jnp.zeros((t, A_V_DIM), F32))

    carry = lax.fori_loop(0, qi, body, (init(), init()))

    js = pl.multiple_of(qi * t, t)
    vj = v_ref[pl.ds(js, t), :]
    keep = lax.broadcasted_iota(jnp.int32, (t, t), 0) >= lax.broadcasted_iota(jnp.int32, (t, t), 1)
    cb = slope * col.astype(F32)
    outs = []
    for mi in range(2):
        s = jnp.where(keep, scores(mi, js) + cb, NEG_BIG)
        _, l, acc = update(carry[mi], s, vj)
        outs.append(acc * (1.0 / l))

    lp = lp_ref[...]
    lam = (jnp.exp(jnp.sum(lp[0:1] * lp[1:2], axis=-1, keepdims=True))
           - jnp.exp(jnp.sum(lp[2:3] * lp[3:4], axis=-1, keepdims=True)) + lam_init)
    o = outs[0] - lam * outs[1]
    o = o * lax.rsqrt(jnp.mean(jnp.square(o), axis=-1, keepdims=True) + RMS_EPS) * g_ref[...]
    o_ref[...] = (o * (1.0 - lam_init)).astype(o_ref.dtype)


def diff_attention(qkv, slopes2, lam_params, norm_g, *, batch, seq, lam_init, t):
    nq = seq // t
    kb = OFF_AK // A_V_DIM
    vb = OFF_AV // A_V_DIM
    return pl.pallas_call(
        functools.partial(_attn_a_kernel, t=t, lam_init=lam_init, slope0=B_Q_HEADS + C_HEADS),
        grid=(batch, A_HEADS, nq),
        in_specs=[
            pl.BlockSpec(memory_space=pltpu.SMEM),
            pl.BlockSpec((t, A_V_DIM), lambda b, h, i: (b * nq + i, h)),
            pl.BlockSpec((seq, A_V_DIM), lambda b, h, i: (b, kb + h)),
            pl.BlockSpec((seq, A_V_DIM), lambda b, h, i: (b, vb + h)),
            pl.BlockSpec((4, HEAD_DIM), lambda b, h, i: (0, 0)),
            pl.BlockSpec((1, A_V_DIM), lambda b, h, i: (0, 0)),
        ],
        out_specs=pl.BlockSpec((t, A_V_DIM), lambda b, h, i: (b * nq + i, h)),
        out_shape=jax.ShapeDtypeStruct((batch * seq, A_WIDTH), BF16),
        compiler_params=_params(3),
        name="diff_attention",
    )(slopes2, qkv, qkv, qkv, lam_params, norm_g.reshape(1, A_V_DIM))


def _banded_kernel(*refs, hk, g, max_dist, step, slope0, has_sink, want_lse):
    it = iter(refs)
    slopes_ref = next(it)
    sinks_ref = next(it) if has_sink else None
    q_ref, kp_ref, k_ref, vp_ref, v_ref = (next(it) for _ in range(5))
    o_ref = next(it)
    lse_ref = next(it) if want_lse else None

    khs = pl.program_id(2)
    qi = pl.program_id(3)
    tq = q_ref.shape[0]
    nt = (((1,), (1,)), ((), ()))
    row = lax.broadcasted_iota(jnp.int32, (BLOCK, 2 * BLOCK), 0)
    col = lax.broadcasted_iota(jnp.int32, (BLOCK, 2 * BLOCK), 1)
    dist = row + BLOCK - col
    band = (dist >= 0) & (dist <= max_dist)
    first_key = jnp.where(qi > 0, 0, BLOCK)
    band_first = band & (col >= first_key)
    dist_f = (dist * step).astype(F32)

    for sb in range(tq // BLOCK):
        rows = slice(sb * BLOCK, (sb + 1) * BLOCK)
        valid = band_first if sb == 0 else band
        for kh in range(hk):
            kc = slice(kh * HEAD_DIM, (kh + 1) * HEAD_DIM)
            if sb == 0:
                kprev, vprev = kp_ref[:, kc], vp_ref[:, kc]
            else:
                prows = slice((sb - 1) * BLOCK, sb * BLOCK)
                kprev, vprev = k_ref[prows, kc], v_ref[prows, kc]
            kk = jnp.concatenate([kprev, k_ref[rows, kc]], axis=0)
            vv = jnp.concatenate([vprev, v_ref[rows, kc]], axis=0)
            qs = [q_ref[rows, (kh * g + gi) * HEAD_DIM:(kh * g + gi + 1) * HEAD_DIM] for gi in range(g)]
            s_all = lax.dot_general(jnp.concatenate(qs, axis=0), kk, nt, preferred_element_type=F32)
            es, invs = [], []
            for gi in range(g):
                head = (khs * hk + kh) * g + gi
                slope = slopes_ref[slope0 + head]
                s = s_all[gi * BLOCK:(gi + 1) * BLOCK] - slope * dist_f
                s = jnp.where(valid, s, NEG_BIG)
                m = jnp.max(s, axis=-1, keepdims=True)
                if has_sink:
                    sk = sinks_ref[head]
                    m = jnp.maximum(m, sk)
                e = jnp.exp2(s - m)
                den = jnp.sum(e, axis=-1, keepdims=True)
                if has_sink:
                    den = den + jnp.exp2(sk - m)
                es.append(e.astype(BF16))
                invs.append(1.0 / den)
                if want_lse:
                    lse = (m + jnp.log2(den)) * LN2
                    lse_ref[rows, (kh * g + gi) * HEAD_DIM:(kh * g + gi + 1) * HEAD_DIM] = (
                        jnp.broadcast_to(lse, (BLOCK, HEAD_DIM)))
            o_all = jnp.dot(jnp.concatenate(es, axis=0), vv, preferred_element_type=F32)
            for gi in range(g):
                o = o_all[gi * BLOCK:(gi + 1) * BLOCK] * invs[gi]
                o_ref[rows, (kh * g + gi) * HEAD_DIM:(kh * g + gi + 1) * HEAD_DIM] = o.astype(o_ref.dtype)


def banded_attention(qkv, slopes2, sinks2, *, batch, seq, dil, tq, hk, g, n_steps, q_off, k_off, v_off,
                     max_dist, slope0, out_width, out_dtype, want_lse):
    n = seq // dil
    nq = n // tq
    qw = hk * g * HEAD_DIM
    kw = hk * HEAD_DIM
    tokens, width = qkv.shape
    view = qkv.reshape(tokens // dil, dil * width)
    qcb, kcb, vcb = q_off // qw, k_off // kw, v_off // kw
    assert q_off % qw == 0 and k_off % kw == 0 and v_off % kw == 0 and width % qw == 0 and width % kw == 0
    rpb = tq // BLOCK

    def prev_row(b, qi):
        return jnp.maximum(b * (n // BLOCK) + qi * rpb - 1, 0)

    has_sink = sinks2 is not None
    in_specs = [pl.BlockSpec(memory_space=pltpu.SMEM)]
    args = [slopes2]
    if has_sink:
        in_specs.append(pl.BlockSpec(memory_space=pltpu.SMEM))
        args.append(sinks2)
    in_specs += [
        pl.BlockSpec((tq, qw), lambda b, r, s, i: (b * nq + i, r * (width // qw) + qcb + s)),
        pl.BlockSpec((BLOCK, kw), lambda b, r, s, i: (prev_row(b, i), r * (width // kw) + kcb + s)),
        pl.BlockSpec((tq, kw), lambda b, r, s, i: (b * nq + i, r * (width // kw) + kcb + s)),
        pl.BlockSpec((BLOCK, kw), lambda b, r, s, i: (prev_row(b, i), r * (width // kw) + vcb + s)),
        pl.BlockSpec((tq, kw), lambda b, r, s, i: (b * nq + i, r * (width // kw) + vcb + s)),
    ]
    args += [view] * 5
    ocb = out_width // qw
    out_spec = pl.BlockSpec((tq, qw), lambda b, r, s, i: (b * nq + i, r * ocb + s))
    out_shapes = [jax.ShapeDtypeStruct((tokens // dil, dil * out_width), out_dtype)]
    out_specs = [out_spec]
    if want_lse:
        out_shapes.append(jax.ShapeDtypeStruct((tokens // dil, dil * out_width), F32))
        out_specs.append(out_spec)
    res = pl.pallas_call(
        functools.partial(_banded_kernel, hk=hk, g=g, max_dist=max_dist, step=dil, slope0=slope0,
                          has_sink=has_sink, want_lse=want_lse),
        grid=(batch, dil, n_steps, nq),
        in_specs=in_specs,
        out_specs=out_specs,
        out_shape=out_shapes,
        compiler_params=_params(4),
        name=f"banded_attention_d{dil}_g{g}",
    )(*args)
    return [r.reshape(tokens, out_width) for r in res]


def _merge_kernel(o0_ref, o1_ref, o2_ref, l0_ref, l1_ref, l2_ref, out_ref):
    l0, l1, l2 = l0_ref[...], l1_ref[...], l2_ref[...]
    mx = jnp.maximum(jnp.maximum(l0, l1), l2)
    e0, e1, e2 = jnp.exp(l0 - mx), jnp.exp(l1 - mx), jnp.exp(l2 - mx)
    inv = 1.0 / (e0 + e1 + e2)
    gw = o0_ref.shape[1]
    out_ref[:, 0:gw] = (e0 * inv * o0_ref[...]).astype(out_ref.dtype)
    out_ref[:, gw:2 * gw] = (e1 * inv * o1_ref[...]).astype(out_ref.dtype)
    out_ref[:, 2 * gw:3 * gw] = (e2 * inv * o2_ref[...]).astype(out_ref.dtype)


def merge_groups(outs, lses, tm):
    tokens, gw = outs[0].shape
    spec = pl.BlockSpec((tm, gw), lambda i: (i, 0))
    return pl.pallas_call(
        _merge_kernel,
        grid=(tokens // tm,),
        in_specs=[spec] * 6,
        out_specs=pl.BlockSpec((tm, 3 * gw), lambda i: (i, 0)),
        out_shape=jax.ShapeDtypeStruct((tokens, 3 * gw), BF16),
        compiler_params=_params(1),
        name="merge_dilation_groups",
    )(*outs, *lses)


def _branch_kernel(oa_ref, ob_ref, oc_ref, ga_ref, gb_ref, gc_ref, w_ref, y_ref, wb_ref):
    @pl.when(pl.program_id(1) == 0)
    def _():
        _cast_weight(w_ref, wb_ref)

    tm = oa_ref.shape[0]
    ka, kb = oa_ref.shape[1], ob_ref.shape[1]
    for c in range(tm // ROW_CHUNK):
        rows = slice(c * ROW_CHUNK, (c + 1) * ROW_CHUNK)
        ya = jnp.dot(oa_ref[rows, :], wb_ref[0:ka, :], preferred_element_type=F32)
        yb = jnp.dot(ob_ref[rows, :], wb_ref[ka:ka + kb, :], preferred_element_type=F32)
        yc = jnp.dot(oc_ref[rows, :], wb_ref[ka + kb:, :], preferred_element_type=F32)
        y = (ga_ref[rows, :].astype(F32) * ya + gb_ref[rows, :].astype(F32) * yb
             + gc_ref[rows, :].astype(F32) * yc)
        y_ref[rows, :] = y.astype(y_ref.dtype)


def branch_projection(o_a, o_b, o_c, gates, w_branch, layer, *, tm, tn):
    m = o_a.shape[0]
    kmix, d = w_branch.shape[-2:]
    nb = d // tn
    return pl.pallas_call(
        _branch_kernel,
        grid=(nb, m // tm),
        in_specs=[
            pl.BlockSpec((tm, o_a.shape[1]), lambda n, i: (i, 0)),
            pl.BlockSpec((tm, o_b.shape[1]), lambda n, i: (i, 0)),
            pl.BlockSpec((tm, o_c.shape[1]), lambda n, i: (i, 0)),
            pl.BlockSpec((tm, tn), lambda n, i: (i, n)),
            pl.BlockSpec((tm, tn), lambda n, i: (i, nb + n)),
            pl.BlockSpec((tm, tn), lambda n, i: (i, 2 * nb + n)),
            pl.BlockSpec((None, kmix, tn), lambda n, i: (layer, 0, n)),
        ],
        out_specs=pl.BlockSpec((tm, tn), lambda n, i: (i, n)),
        out_shape=jax.ShapeDtypeStruct((m, d), BF16),
        scratch_shapes=[pltpu.VMEM((kmix, tn), BF16)],
        compiler_params=_params(2),
        name="branch_projection",
    )(o_a, o_b, o_c, gates, gates, gates, w_branch)


def _resid_mm_kernel(y_ref, w_ref, x_ref, o_ref, wb_ref, *, alpha):
    @pl.when(pl.program_id(1) == 0)
    def _():
        _cast_weight(w_ref, wb_ref)

    tm = y_ref.shape[0]
    for c in range(tm // ROW_CHUNK):
        rows = slice(c * ROW_CHUNK, (c + 1) * ROW_CHUNK)
        acc = jnp.dot(y_ref[rows, :], wb_ref[...], preferred_element_type=F32)
        o_ref[rows, :] = alpha * x_ref[rows, :] + acc


def residual_matmul(y, w, layer, x, *, alpha, tm, tn):
    m, k = y.shape
    d = w.shape[-1]
    return pl.pallas_call(
        functools.partial(_resid_mm_kernel, alpha=alpha),
        grid=(d // tn, m // tm),
        in_specs=[
            pl.BlockSpec((tm, k), lambda n, i: (i, 0)),
            pl.BlockSpec((None, k, tn), lambda n, i: (layer, 0, n)),
            pl.BlockSpec((tm, tn), lambda n, i: (i, n)),
        ],
        out_specs=pl.BlockSpec((tm, tn), lambda n, i: (i, n)),
        out_shape=jax.ShapeDtypeStruct((m, d), F32),
        scratch_shapes=[pltpu.VMEM((k, tn), BF16)],
        compiler_params=_params(2),
        name="out_projection",
    )(y, w, x)


def _resid_mm_bf16_kernel(a_ref, w_ref, x_ref, o_ref, *, alpha):
    tm = a_ref.shape[0]
    for c in range(tm // ROW_CHUNK):
        rows = slice(c * ROW_CHUNK, (c + 1) * ROW_CHUNK)
        acc = jnp.dot(a_ref[rows, :], w_ref[...], preferred_element_type=F32)
        o_ref[rows, :] = alpha * x_ref[rows, :] + acc


def residual_matmul_bf16(a, wb, x, *, alpha, tm, tn):
    m, k = a.shape
    d = wb.shape[1]
    return pl.pallas_call(
        functools.partial(_resid_mm_bf16_kernel, alpha=alpha),
        grid=(m // tm, d // tn),
        in_specs=[
            pl.BlockSpec((tm, k), lambda i, n: (i, 0)),
            pl.BlockSpec((k, tn), lambda i, n: (0, n)),
            pl.BlockSpec((tm, tn), lambda i, n: (i, n)),
        ],
        out_specs=pl.BlockSpec((tm, tn), lambda i, n: (i, n)),
        out_shape=jax.ShapeDtypeStruct((m, d), F32),
        compiler_params=_params(2),
        name="down_projection",
    )(a, wb, x)


def _ln_kernel(x_ref, g_ref, b_ref, o_ref, ob_ref):
    x = x_ref[...]
    mu = jnp.mean(x, axis=-1, keepdims=True)
    xc = x - mu
    var = jnp.mean(jnp.square(xc), axis=-1, keepdims=True)
    y = xc * lax.rsqrt(var + LN_EPS) * g_ref[...] + b_ref[...]
    o_ref[...] = y
    ob_ref[...] = y.astype(ob_ref.dtype)


def layer_norm(x, g, b, *, tm):
    m, d = x.shape
    spec = pl.BlockSpec((tm, d), lambda i: (i, 0))
    vec = pl.BlockSpec((1, d), lambda i: (0, 0))
    return pl.pallas_call(
        _ln_kernel,
        grid=(m // tm,),
        in_specs=[spec, vec, vec],
        out_specs=[spec, spec],
        out_shape=[jax.ShapeDtypeStruct((m, d), F32), jax.ShapeDtypeStruct((m, d), BF16)],
        compiler_params=_params(1),
        name="layer_norm",
    )(x, g.reshape(1, d), b.reshape(1, d))


def _up_kernel(x_ref, wg_ref, wv_ref, cwg_ref, cwv_ref, cbg_ref, cbv_ref, o_ref, wb_ref, halo_ref, *, seq):
    i = pl.program_id(1)
    tm = x_ref.shape[0]
    tn = wg_ref.shape[1]

    @pl.when(i == 0)
    def _():
        _cast_weight(wg_ref, wb_ref, 0)
        _cast_weight(wv_ref, wb_ref, tn)

    @pl.when(lax.rem(i * tm, seq) == 0)
    def _():
        halo_ref[...] = jnp.zeros_like(halo_ref)

    row = lax.broadcasted_iota(jnp.int32, (ROW_CHUNK, tn), 0)
    for c in range(tm // ROW_CHUNK):
        rows = slice(c * ROW_CHUNK, (c + 1) * ROW_CHUNK)
        h = jnp.dot(x_ref[rows, :], wb_ref[...], preferred_element_type=F32)
        halo = halo_ref[...]
        halo_ref[...] = h[ROW_CHUNK - 8:, :]
        acts = []
        for half, (cw_ref, cb_ref) in enumerate(((cwg_ref, cbg_ref), (cwv_ref, cbv_ref))):
            hh = h[:, half * tn:(half + 1) * tn]
            hl = halo[:, half * tn:(half + 1) * tn]
            h1 = jnp.where(row == 0, hl[7:8], pltpu.roll(hh, 1, 0))
            h2 = jnp.where(row == 0, hl[6:7], jnp.where(row == 1, hl[7:8], pltpu.roll(hh, 2, 0)))
            cw = cw_ref[...]
            acts.append(cb_ref[...] + cw[0:1] * h2 + cw[1:2] * h1 + cw[2:3] * hh)
        gate, val = acts
        o_ref[rows, :] = (gate * jax.nn.sigmoid(gate) * val).astype(o_ref.dtype)


def up_conv_glu(xb, w_up, conv_w, conv_b, layer, *, seq, tm, tn):
    m, k = xb.shape
    dff = w_up.shape[-1] // 2
    nb = dff // tn
    assert seq % tm == 0 and dff % tn == 0
    conv_b = conv_b.reshape(conv_b.shape[0], 1, 2 * dff)
    return pl.pallas_call(
        functools.partial(_up_kernel, seq=seq),
        grid=(nb, m // tm),
        in_specs=[
            pl.BlockSpec((tm, k), lambda n, i: (i, 0)),
            pl.BlockSpec((None, k, tn), lambda n, i: (layer, 0, n)),
            pl.BlockSpec((None, k, tn), lambda n, i: (layer, 0, nb + n)),
            pl.BlockSpec((None, CONV_WIDTH, tn), lambda n, i: (layer, 0, n)),
            pl.BlockSpec((None, CONV_WIDTH, tn), lambda n, i: (layer, 0, nb + n)),
            pl.BlockSpec((None, 1, tn), lambda n, i: (layer, 0, n)),
            pl.BlockSpec((None, 1, tn), lambda n, i: (layer, 0, nb + n)),
        ],
        out_specs=pl.BlockSpec((tm, tn), lambda n, i: (i, n)),
        out_shape=jax.ShapeDtypeStruct((m, dff), BF16),
        scratch_shapes=[pltpu.VMEM((k, 2 * tn), BF16), pltpu.VMEM((8, 2 * tn), F32)],
        compiler_params=_params(2),
        name="up_conv_glu",
    )(xb, w_up, w_up, conv_w, conv_w, conv_b, conv_b)


class _Tiles:
    mm_rows = 1024
    qkv_cols = 768
    gate_cols = 512
    branch_cols = 512
    out_cols = 512
    up_cols = 256
    down_rows = 512
    down_cols = 512
    attn_a = 256
    banded_rows = 512
    ln_rows = 256
    cast_rows = 256


def _forward(x, w_in, diff_lambda, diff_norm_g, sink_logits, w_branch, w_o, ln1_g, ln1_b,
             w_up, conv_w, conv_b, w_down, ln2_g, ln2_b, tiles=_Tiles):
    batch, seq, d = x.shape
    depth = w_in.shape[0]
    tokens = batch * seq
    alpha = (2 * depth) ** 0.25

    slopes = jnp.exp2(-8.0 * (jnp.arange(N_ALIBI_HEADS, dtype=F32) + 1.0) / N_ALIBI_HEADS)
    slopes2 = slopes * LOG2E
    colscale = np.ones((1, QKV_WIDTH), np.float32)
    for off, wdt in ((OFF_AQ, A_WIDTH), (OFF_BQ, B_WIDTH), (OFF_CQ, C_WIDTH)):
        colscale[:, off:off + wdt] = SCALE * LOG2E
    colscale = jnp.asarray(colscale)

    x32 = x.reshape(tokens, d)
    xb = cast_bf16(x32, tiles.cast_rows)
    for l in range(depth):
        lam_init = 0.8 - 0.6 * math.exp(-0.3 * l)
        qkv = in_projection(xb, w_in, l, 0, QKV_WIDTH, colscale=colscale,
                            tm=tiles.mm_rows, tn=tiles.qkv_cols)
        gates = in_projection(xb, w_in, l, QKV_WIDTH, 3 * d, colscale=None,
                              tm=tiles.mm_rows, tn=tiles.gate_cols)
        o_a = diff_attention(qkv, slopes2, diff_lambda[l], diff_norm_g[l], batch=batch, seq=seq,
                             lam_init=lam_init, t=tiles.attn_a)
        (o_b,) = banded_attention(
            qkv, slopes2, sink_logits[l] * LOG2E, batch=batch, seq=seq, dil=1, tq=tiles.banded_rows,
            hk=1, g=B_Q_HEADS // B_KV_HEADS, n_steps=B_KV_HEADS, q_off=OFF_BQ, k_off=OFF_BK, v_off=OFF_BV,
            max_dist=B_WINDOW - 1, slope0=0, out_width=B_WIDTH, out_dtype=BF16, want_lse=False)
        c_outs, c_lses = [], []
        gw = C_HEADS_PER_GROUP * HEAD_DIM
        for gi, (window, dil) in enumerate(C_GROUPS):
            o, lse = banded_attention(
                qkv, slopes2, None, batch=batch, seq=seq, dil=dil, tq=min(tiles.banded_rows, seq // dil),
                hk=C_HEADS_PER_GROUP, g=1, n_steps=1, q_off=OFF_CQ + gi * gw, k_off=OFF_CK + gi * gw,
                v_off=OFF_CV + gi * gw, max_dist=window // dil, slope0=B_Q_HEADS + gi * C_HEADS_PER_GROUP,
                out_width=gw, out_dtype=F32, want_lse=True)
            c_outs.append(o)
            c_lses.append(lse)
        o_c = merge_groups(c_outs, c_lses, tiles.ln_rows)
        y = branch_projection(o_a, o_b, o_c, gates, w_branch, l, tm=tiles.mm_rows, tn=tiles.branch_cols)
        pre = residual_matmul(y, w_o, l, x32, alpha=alpha, tm=tiles.mm_rows, tn=tiles.out_cols)
        x32, xb = layer_norm(pre, ln1_g[l], ln1_b[l], tm=tiles.ln_rows)
        act = up_conv_glu(xb, w_up, conv_w, conv_b, l, seq=seq, tm=tiles.mm_rows, tn=tiles.up_cols)
        wdb = cast_bf16(w_down, tiles.cast_rows, layer=l)
        pre = residual_matmul_bf16(act, wdb, x32, alpha=alpha, tm=tiles.down_rows, tn=tiles.down_cols)
        x32, xb = layer_norm(pre, ln2_g[l], ln2_b[l], tm=tiles.ln_rows)
    return x32.reshape(batch, seq, d)


def kernel(x, w_in, diff_lambda, diff_norm_g, sink_logits, w_branch, w_o, ln1_g, ln1_b,
           w_up, conv_w, conv_b, w_down, ln2_g, ln2_b):
    return _forward(x, w_in, diff_lambda, diff_norm_g, sink_logits, w_branch, w_o, ln1_g, ln1_b,
                    w_up, conv_w, conv_b, w_down, ln2_g, ln2_b)
```

```python
import functools
import math

import numpy as np
import jax
import jax.numpy as jnp
from jax import lax
from jax.experimental import pallas as pl
from jax.experimental.pallas import tpu as pltpu

F32 = jnp.float32
BF16 = jnp.bfloat16

HEAD_DIM = 128
BLOCK = 128
SCALE = HEAD_DIM ** -0.5
A_HEADS = 6
A_V_DIM = 2 * HEAD_DIM
B_Q_HEADS = 8
B_KV_HEADS = 2
B_WINDOW = 128
C_GROUPS = ((128, 1), (512, 4), (2048, 16))
C_HEADS_PER_GROUP = 4
C_HEADS = C_HEADS_PER_GROUP * len(C_GROUPS)
N_ALIBI_HEADS = B_Q_HEADS + C_HEADS + A_HEADS
A_WIDTH = A_HEADS * A_V_DIM
B_WIDTH = B_Q_HEADS * HEAD_DIM
C_WIDTH = C_HEADS * HEAD_DIM
CONV_WIDTH = 3
LN_EPS = 1e-5
RMS_EPS = 1e-5
LOG2E = math.log2(math.e)
LN2 = math.log(2.0)
NEG_BIG = -1e30

OFF_AQ = 0
OFF_AK = OFF_AQ + A_WIDTH
OFF_AV = OFF_AK + A_WIDTH
OFF_BQ = OFF_AV + A_WIDTH
OFF_BK = OFF_BQ + B_WIDTH
OFF_BV = OFF_BK + B_KV_HEADS * HEAD_DIM
OFF_CQ = OFF_BV + B_KV_HEADS * HEAD_DIM
OFF_CK = OFF_CQ + C_WIDTH
OFF_CV = OFF_CK + C_WIDTH
QKV_WIDTH = OFF_CV + C_WIDTH

V7X_VMEM_LIMIT_BYTES = 60 * 1024 * 1024
ROW_CHUNK = 256


def _params(n_axes, vmem=V7X_VMEM_LIMIT_BYTES):
    return pltpu.CompilerParams(dimension_semantics=("arbitrary",) * n_axes, vmem_limit_bytes=vmem)


def _stream_weights(w_hbm, wb_ref, stage_ref, sems, *, layer, col_starts):
    n, m = pl.program_id(0), pl.program_id(1)
    n_blocks, n_tiles = pl.num_programs(0), pl.num_programs(1)
    k = wb_ref.shape[1]
    rows, width = stage_ref.shape
    assert rows * n_tiles == k
    pieces = len(col_starts(0))
    pw = width // pieces

    def copies(block, row0):
        return [pltpu.make_async_copy(
                    w_hbm.at[layer, pl.ds(row0, rows), pl.ds(c0, pw)],
                    stage_ref.at[:, pl.ds(p * pw, pw)], sems.at[p])
                for p, c0 in enumerate(col_starts(block))]

    @pl.when(jnp.logical_and(n == 0, m == 0))
    def _():
        def body(s, c):
            row0 = pl.multiple_of(s * rows, rows)
            for cp in copies(0, row0):
                cp.start()
            for cp in copies(0, row0):
                cp.wait()
            wb_ref[0, pl.ds(row0, rows), :] = stage_ref[...].astype(BF16)
            return c

        lax.fori_loop(0, n_tiles, body, 0)

    has_next = n + 1 < n_blocks
    row0 = pl.multiple_of(m * rows, rows)

    @pl.when(has_next)
    def _():
        for cp in copies(n + 1, row0):
            cp.start()

    def finish():
        @pl.when(has_next)
        def _():
            for cp in copies(n + 1, row0):
                cp.wait()
            wb_ref[lax.rem(n + 1, 2), pl.ds(row0, rows), :] = stage_ref[...].astype(BF16)

    return lax.rem(n, 2), finish


def _weight_scratch(k, width, n_tiles, pieces=1):
    return [pltpu.VMEM((2, k, width), BF16), pltpu.VMEM((k // n_tiles, width), F32),
            pltpu.SemaphoreType.DMA((pieces,))]


def _cast_kernel(x_ref, o_ref):
    o_ref[...] = x_ref[...].astype(o_ref.dtype)


def cast_bf16(x, rows, layer=None):
    r, c = x.shape[-2:]
    if layer is None:
        in_spec = pl.BlockSpec((rows, c), lambda i: (i, 0))
    else:
        in_spec = pl.BlockSpec((None, rows, c), lambda i: (layer, i, 0))
    return pl.pallas_call(
        _cast_kernel,
        grid=(r // rows,),
        in_specs=[in_spec],
        out_specs=pl.BlockSpec((rows, c), lambda i: (i, 0)),
        out_shape=jax.ShapeDtypeStruct((r, c), BF16),
        compiler_params=_params(1),
        name="cast_bf16",
    )(x)


def _cast_weight(w_ref, wb_ref):
    def body(i, c):
        r = pl.multiple_of(i * ROW_CHUNK, ROW_CHUNK)
        wb_ref[pl.ds(r, ROW_CHUNK), :] = w_ref[pl.ds(r, ROW_CHUNK), :].astype(BF16)
        return c

    lax.fori_loop(0, w_ref.shape[0] // ROW_CHUNK, body, 0)


def _inproj_scale_kernel(x_ref, w_ref, cs_ref, o_ref, wb_ref):
    @pl.when(pl.program_id(1) == 0)
    def _():
        _cast_weight(w_ref, wb_ref)

    for c in range(x_ref.shape[0] // ROW_CHUNK):
        rows = slice(c * ROW_CHUNK, (c + 1) * ROW_CHUNK)
        acc = jnp.dot(x_ref[rows, :], wb_ref[...], preferred_element_type=F32)
        o_ref[rows, :] = (acc * cs_ref[...]).astype(o_ref.dtype)


def _inproj_sigmoid_kernel(x_ref, w_ref, o_ref, wb_ref):
    @pl.when(pl.program_id(1) == 0)
    def _():
        _cast_weight(w_ref, wb_ref)

    for c in range(x_ref.shape[0] // ROW_CHUNK):
        rows = slice(c * ROW_CHUNK, (c + 1) * ROW_CHUNK)
        acc = jnp.dot(x_ref[rows, :], wb_ref[...], preferred_element_type=F32)
        o_ref[rows, :] = jax.nn.sigmoid(acc).astype(o_ref.dtype)


def in_projection(xb, w, layer, col0, width, *, colscale, out_dtype, tm, tn):
    m, k = xb.shape
    assert col0 % tn == 0 and width % tn == 0 and m % tm == 0
    cb = col0 // tn
    in_specs = [
        pl.BlockSpec((tm, k), lambda n, i: (i, 0)),
        pl.BlockSpec((None, k, tn), lambda n, i: (layer, 0, cb + n)),
    ]
    args = [xb, w]
    if colscale is not None:
        in_specs.append(pl.BlockSpec((1, tn), lambda n, i: (0, n)))
        args.append(colscale)
    return pl.pallas_call(
        _inproj_sigmoid_kernel if colscale is None else _inproj_scale_kernel,
        grid=(width // tn, m // tm),
        in_specs=in_specs,
        out_specs=pl.BlockSpec((tm, tn), lambda n, i: (i, n)),
        out_shape=jax.ShapeDtypeStruct((m, width), out_dtype),
        scratch_shapes=[pltpu.VMEM((k, tn), BF16)],
        compiler_params=_params(2),
        name="inproj_gates" if colscale is None else "inproj_qkv",
    )(*args)


def _lane_tile(x, n):
    return jnp.concatenate([x] * n, axis=1)


def _attn_a_kernel(slopes_ref, q_ref, k_ref, v_ref, lp_ref, g_ref, o_ref, m_ref, l_ref, acc_ref,
                   *, tq, tk, lam_init, slope0):
    h = pl.program_id(1)
    qi = pl.program_id(2)
    slope = slopes_ref[slope0 + h]
    col = lax.broadcasted_iota(jnp.int32, (1, tk), 1)
    nt = (((1,), (1,)), ((), ()))

    m_ref[...] = jnp.full(m_ref.shape, NEG_BIG, F32)
    l_ref[...] = jnp.zeros(l_ref.shape, F32)
    acc_ref[...] = jnp.zeros(acc_ref.shape, F32)

    def step(r0, js, rel, masked):
        rows = slice(r0, tq)
        vj = v_ref[pl.ds(js, tk), :]
        cb = slope * (col + rel).astype(F32)
        for mi in range(2):
            hd = slice(mi * HEAD_DIM, (mi + 1) * HEAD_DIM)
            s = lax.dot_general(q_ref[rows, hd], k_ref[pl.ds(js, tk), hd], nt, preferred_element_type=F32) + cb
            if masked:
                keep = (lax.broadcasted_iota(jnp.int32, (tq - r0, tk), 0)
                        >= lax.broadcasted_iota(jnp.int32, (tq - r0, tk), 1))
                s = jnp.where(keep, s, NEG_BIG)
            mo = m_ref[mi, rows]
            mn = jnp.maximum(mo, jnp.max(s, axis=-1, keepdims=True))
            a = jnp.exp2(mo - mn)
            p = jnp.exp2(s - _lane_tile(mn, tk // HEAD_DIM))
            l_ref[mi, rows] = a * l_ref[mi, rows] + jnp.sum(p, axis=-1, keepdims=True)
            m_ref[mi, rows] = mn
            acc_ref[mi, rows] = (_lane_tile(a, A_V_DIM // HEAD_DIM) * acc_ref[mi, rows]
                                 + jnp.dot(p.astype(BF16), vj, preferred_element_type=F32))

    def body(j, c):
        js = pl.multiple_of(j * tk, tk)
        step(0, js, js - qi * tq, False)
        return c

    lax.fori_loop(0, qi * (tq // tk), body, 0)
    for jj in range(tq // tk):
        step(jj * tk, pl.multiple_of(qi * tq + jj * tk, tk), jj * tk, True)

    lp = lp_ref[...]
    lam = (jnp.exp(jnp.sum(lp[0:1] * lp[1:2], axis=-1, keepdims=True))
           - jnp.exp(jnp.sum(lp[2:3] * lp[3:4], axis=-1, keepdims=True)) + lam_init)
    rep = A_V_DIM // HEAD_DIM
    o = (acc_ref[0] * _lane_tile(1.0 / l_ref[0], rep)
         - lam * (acc_ref[1] * _lane_tile(1.0 / l_ref[1], rep)))
    o = o * lax.rsqrt(jnp.mean(jnp.square(o), axis=-1, keepdims=True) + RMS_EPS) * g_ref[...]
    o_ref[...] = (o * (1.0 - lam_init)).astype(o_ref.dtype)


def diff_attention(qkv, slopes2, lam_params, norm_g, *, batch, seq, lam_init, tq, tk):
    nq = seq // tq
    kb = OFF_AK // A_V_DIM
    vb = OFF_AV // A_V_DIM
    assert tq % tk == 0 and seq % tq == 0
    return pl.pallas_call(
        functools.partial(_attn_a_kernel, tq=tq, tk=tk, lam_init=lam_init, slope0=B_Q_HEADS + C_HEADS),
        grid=(batch, A_HEADS, nq),
        in_specs=[
            pl.BlockSpec(memory_space=pltpu.SMEM),
            pl.BlockSpec((tq, A_V_DIM), lambda b, h, i: (b * nq + i, h)),
            pl.BlockSpec((seq, A_V_DIM), lambda b, h, i: (b, kb + h)),
            pl.BlockSpec((seq, A_V_DIM), lambda b, h, i: (b, vb + h)),
            pl.BlockSpec((4, HEAD_DIM), lambda b, h, i: (0, 0)),
            pl.BlockSpec((1, A_V_DIM), lambda b, h, i: (0, 0)),
        ],
        out_specs=pl.BlockSpec((tq, A_V_DIM), lambda b, h, i: (b * nq + i, h)),
        out_shape=jax.ShapeDtypeStruct((batch * seq, A_WIDTH), BF16),
        scratch_shapes=[pltpu.VMEM((2, tq, HEAD_DIM), F32), pltpu.VMEM((2, tq, HEAD_DIM), F32),
                        pltpu.VMEM((2, tq, A_V_DIM), F32)],
        compiler_params=_params(3),
        name="diff_attention",
    )(slopes2, qkv, qkv, qkv, lam_params, norm_g.reshape(1, A_V_DIM))


def _banded_kernel(*refs, dil, tq, hk, g, max_dist, slope0, has_sink, want_lse):
    it = iter(refs)
    slopes_ref = next(it)
    sinks_ref = next(it) if has_sink else None
    q_ref, kp_ref, k_ref, vp_ref, v_ref = (next(it) for _ in range(5))
    o_ref = next(it)
    lse_ref = next(it) if want_lse else None

    khs = pl.program_id(1)
    qi = pl.program_id(2)
    nt = (((1,), (1,)), ((), ()))

    def rows_of(blk, r):
        if dil == 1:
            return slice(blk * BLOCK, (blk + 1) * BLOCK)
        return pl.ds(blk * BLOCK * dil + r, BLOCK, stride=dil)

    def residue(r):
        row = lax.broadcasted_iota(jnp.int32, (BLOCK, 2 * BLOCK), 0)
        col = lax.broadcasted_iota(jnp.int32, (BLOCK, 2 * BLOCK), 1)
        dist = row + BLOCK - col
        band = (dist >= 0) & (dist <= max_dist)
        first_key = jnp.where(qi > 0, 0, BLOCK)
        band_first = band & (col >= first_key)
        dist_f = (dist * dil).astype(F32)
        for sb in range(tq // BLOCK):
            rows = rows_of(sb, r)
            valid = band_first if sb == 0 else band
            for kh in range(hk):
                kc = slice(kh * HEAD_DIM, (kh + 1) * HEAD_DIM)
                if sb == 0:
                    kprev, vprev = kp_ref[rows_of(0, r), kc], vp_ref[rows_of(0, r), kc]
                else:
                    kprev, vprev = k_ref[rows_of(sb - 1, r), kc], v_ref[rows_of(sb - 1, r), kc]
                kk = jnp.concatenate([kprev, k_ref[rows, kc]], axis=0).astype(BF16)
                vv = jnp.concatenate([vprev, v_ref[rows, kc]], axis=0).astype(BF16)
                hc = [slice((kh * g + gi) * HEAD_DIM, (kh * g + gi + 1) * HEAD_DIM) for gi in range(g)]
                qs = jnp.concatenate([q_ref[rows, c] for c in hc], axis=0).astype(BF16)
                s_all = lax.dot_general(qs, kk, nt, preferred_element_type=F32)
                es, invs = [], []
                for gi in range(g):
                    head = (khs * hk + kh) * g + gi
                    slope = slopes_ref[slope0 + head]
                    s = s_all[gi * BLOCK:(gi + 1) * BLOCK] - slope * dist_f
                    s = jnp.where(valid, s, NEG_BIG)
                    m = jnp.max(s, axis=-1, keepdims=True)
                    if has_sink:
                        sk = sinks_ref[head]
                        m = jnp.maximum(m, sk)
                    e = jnp.exp2(s - m)
                    den = jnp.sum(e, axis=-1, keepdims=True)
                    if has_sink:
                        den = den + jnp.exp2(sk - m)
                    es.append(e.astype(BF16))
                    invs.append(1.0 / den)
                    if want_lse:
                        lse = (m + jnp.log2(den)) * LN2
                        lse_ref[rows, hc[gi]] = jnp.broadcast_to(lse, (BLOCK, HEAD_DIM))
                o_all = jnp.dot(jnp.concatenate(es, axis=0), vv, preferred_element_type=F32)
                for gi in range(g):
                    o = o_all[gi * BLOCK:(gi + 1) * BLOCK] * invs[gi]
                    o_ref[rows, hc[gi]] = o.astype(o_ref.dtype)

    if dil == 1:
        residue(0)
    else:
        def body(r, c):
            residue(r)
            return c

        lax.fori_loop(0, dil, body, 0)


def banded_attention(qkv, slopes2, sinks2, *, batch, seq, dil, tq, hk, g, n_steps, q_off, k_off, v_off,
                     max_dist, slope0, out_width, out_dtype, want_lse):
    tokens = qkv.shape[0]
    tt = tq * dil
    pt = BLOCK * dil
    nq = seq // tt
    qw = hk * g * HEAD_DIM
    kw = hk * HEAD_DIM
    qcb, kcb, vcb = q_off // qw, k_off // kw, v_off // kw
    assert q_off % qw == 0 and k_off % kw == 0 and v_off % kw == 0 and seq % tt == 0 and tt % pt == 0

    def prev_row(b, i):
        return jnp.maximum(b * (seq // pt) + i * (tt // pt) - 1, 0)

    has_sink = sinks2 is not None
    in_specs = [pl.BlockSpec(memory_space=pltpu.SMEM)]
    args = [slopes2]
    if has_sink:
        in_specs.append(pl.BlockSpec(memory_space=pltpu.SMEM))
        args.append(sinks2)
    in_specs += [
        pl.BlockSpec((tt, qw), lambda b, s, i: (b * nq + i, qcb + s)),
        pl.BlockSpec((pt, kw), lambda b, s, i: (prev_row(b, i), kcb + s)),
        pl.BlockSpec((tt, kw), lambda b, s, i: (b * nq + i, kcb + s)),
        pl.BlockSpec((pt, kw), lambda b, s, i: (prev_row(b, i), vcb + s)),
        pl.BlockSpec((tt, kw), lambda b, s, i: (b * nq + i, vcb + s)),
    ]
    args += [qkv] * 5
    out_spec = pl.BlockSpec((tt, qw), lambda b, s, i: (b * nq + i, s))
    out_shapes = [jax.ShapeDtypeStruct((tokens, out_width), out_dtype)]
    out_specs = [out_spec]
    if want_lse:
        out_shapes.append(jax.ShapeDtypeStruct((tokens, out_width), F32))
        out_specs.append(out_spec)
    return pl.pallas_call(
        functools.partial(_banded_kernel, dil=dil, tq=tq, hk=hk, g=g, max_dist=max_dist, slope0=slope0,
                          has_sink=has_sink, want_lse=want_lse),
        grid=(batch, n_steps, nq),
        in_specs=in_specs,
        out_specs=out_specs,
        out_shape=out_shapes,
        compiler_params=_params(3),
        name=f"banded_attention_d{dil}_g{g}",
    )(*args)


def _merge_kernel(o0_ref, o1_ref, o2_ref, l0_ref, l1_ref, l2_ref, out_ref):
    l0, l1, l2 = l0_ref[...], l1_ref[...], l2_ref[...]
    mx = jnp.maximum(jnp.maximum(l0, l1), l2)
    e0, e1, e2 = jnp.exp(l0 - mx), jnp.exp(l1 - mx), jnp.exp(l2 - mx)
    inv = 1.0 / (e0 + e1 + e2)
    gw = o0_ref.shape[1]
    out_ref[:, 0:gw] = (e0 * inv * o0_ref[...]).astype(out_ref.dtype)
    out_ref[:, gw:2 * gw] = (e1 * inv * o1_ref[...]).astype(out_ref.dtype)
    out_ref[:, 2 * gw:3 * gw] = (e2 * inv * o2_ref[...]).astype(out_ref.dtype)


def merge_groups(outs, lses, tm):
    tokens, gw = outs[0].shape
    spec = pl.BlockSpec((tm, gw), lambda i: (i, 0))
    return pl.pallas_call(
        _merge_kernel,
        grid=(tokens // tm,),
        in_specs=[spec] * 6,
        out_specs=pl.BlockSpec((tm, 3 * gw), lambda i: (i, 0)),
        out_shape=jax.ShapeDtypeStruct((tokens, 3 * gw), BF16),
        compiler_params=_params(1),
        name="merge_dilation_groups",
    )(*outs, *lses)


def _branch_kernel(oa_ref, ob_ref, oc_ref, ga_ref, gb_ref, gc_ref, w_ref, y_ref, wb_ref):
    @pl.when(pl.program_id(1) == 0)
    def _():
        _cast_weight(w_ref, wb_ref)

    tm = oa_ref.shape[0]
    ka, kb = oa_ref.shape[1], ob_ref.shape[1]
    for c in range(tm // ROW_CHUNK):
        rows = slice(c * ROW_CHUNK, (c + 1) * ROW_CHUNK)
        ya = jnp.dot(oa_ref[rows, :], wb_ref[0:ka, :], preferred_element_type=F32)
        yb = jnp.dot(ob_ref[rows, :], wb_ref[ka:ka + kb, :], preferred_element_type=F32)
        yc = jnp.dot(oc_ref[rows, :], wb_ref[ka + kb:, :], preferred_element_type=F32)
        y = (ga_ref[rows, :].astype(F32) * ya + gb_ref[rows, :].astype(F32) * yb
             + gc_ref[rows, :].astype(F32) * yc)
        y_ref[rows, :] = y.astype(y_ref.dtype)


def branch_projection(o_a, o_b, o_c, gates, w_branch, layer, *, tm, tn):
    m = o_a.shape[0]
    kmix, d = w_branch.shape[-2:]
    nb = d // tn
    return pl.pallas_call(
        _branch_kernel,
        grid=(nb, m // tm),
        in_specs=[
            pl.BlockSpec((tm, o_a.shape[1]), lambda n, i: (i, 0)),
            pl.BlockSpec((tm, o_b.shape[1]), lambda n, i: (i, 0)),
            pl.BlockSpec((tm, o_c.shape[1]), lambda n, i: (i, 0)),
            pl.BlockSpec((tm, tn), lambda n, i: (i, n)),
            pl.BlockSpec((tm, tn), lambda n, i: (i, nb + n)),
            pl.BlockSpec((tm, tn), lambda n, i: (i, 2 * nb + n)),
            pl.BlockSpec((None, kmix, tn), lambda n, i: (layer, 0, n)),
        ],
        out_specs=pl.BlockSpec((tm, tn), lambda n, i: (i, n)),
        out_shape=jax.ShapeDtypeStruct((m, d), BF16),
        scratch_shapes=[pltpu.VMEM((kmix, tn), BF16)],
        compiler_params=_params(2),
        name="branch_projection",
    )(o_a, o_b, o_c, gates, gates, gates, w_branch)


def _resid_mm_kernel(y_ref, w_ref, x_ref, o_ref, wb_ref, *, alpha):
    @pl.when(pl.program_id(1) == 0)
    def _():
        _cast_weight(w_ref, wb_ref)

    for c in range(y_ref.shape[0] // ROW_CHUNK):
        rows = slice(c * ROW_CHUNK, (c + 1) * ROW_CHUNK)
        acc = jnp.dot(y_ref[rows, :], wb_ref[...], preferred_element_type=F32)
        o_ref[rows, :] = alpha * x_ref[rows, :] + acc


def residual_matmul(y, w, layer, x, *, alpha, tm, tn):
    m, k = y.shape
    d = w.shape[-1]
    return pl.pallas_call(
        functools.partial(_resid_mm_kernel, alpha=alpha),
        grid=(d // tn, m // tm),
        in_specs=[
            pl.BlockSpec((tm, k), lambda n, i: (i, 0)),
            pl.BlockSpec((None, k, tn), lambda n, i: (layer, 0, n)),
            pl.BlockSpec((tm, tn), lambda n, i: (i, n)),
        ],
        out_specs=pl.BlockSpec((tm, tn), lambda n, i: (i, n)),
        out_shape=jax.ShapeDtypeStruct((m, d), F32),
        scratch_shapes=[pltpu.VMEM((k, tn), BF16)],
        compiler_params=_params(2),
        name="out_projection",
    )(y, w, x)


def _resid_mm_bf16_kernel(a_ref, w_ref, x_ref, o_ref, *, alpha):
    tm = a_ref.shape[0]
    for c in range(tm // ROW_CHUNK):
        rows = slice(c * ROW_CHUNK, (c + 1) * ROW_CHUNK)
        acc = jnp.dot(a_ref[rows, :], w_ref[...], preferred_element_type=F32)
        o_ref[rows, :] = alpha * x_ref[rows, :] + acc


def residual_matmul_bf16(a, wb, x, *, alpha, tm, tn):
    m, k = a.shape
    d = wb.shape[1]
    return pl.pallas_call(
        functools.partial(_resid_mm_bf16_kernel, alpha=alpha),
        grid=(m // tm, d // tn),
        in_specs=[
            pl.BlockSpec((tm, k), lambda i, n: (i, 0)),
            pl.BlockSpec((k, tn), lambda i, n: (0, n)),
            pl.BlockSpec((tm, tn), lambda i, n: (i, n)),
        ],
        out_specs=pl.BlockSpec((tm, tn), lambda i, n: (i, n)),
        out_shape=jax.ShapeDtypeStruct((m, d), F32),
        compiler_params=_params(2),
        name="down_projection",
    )(a, wb, x)


def _ln_kernel(x_ref, g_ref, b_ref, o_ref, ob_ref):
    x = x_ref[...]
    mu = jnp.mean(x, axis=-1, keepdims=True)
    xc = x - mu
    var = jnp.mean(jnp.square(xc), axis=-1, keepdims=True)
    y = xc * lax.rsqrt(var + LN_EPS) * g_ref[...] + b_ref[...]
    o_ref[...] = y
    ob_ref[...] = y.astype(ob_ref.dtype)


def layer_norm(x, g, b, *, tm):
    m, d = x.shape
    spec = pl.BlockSpec((tm, d), lambda i: (i, 0))
    vec = pl.BlockSpec((1, d), lambda i: (0, 0))
    return pl.pallas_call(
        _ln_kernel,
        grid=(m // tm,),
        in_specs=[spec, vec, vec],
        out_specs=[spec, spec],
        out_shape=[jax.ShapeDtypeStruct((m, d), F32), jax.ShapeDtypeStruct((m, d), BF16)],
        compiler_params=_params(1),
        name="layer_norm",
    )(x, g.reshape(1, d), b.reshape(1, d))


def _up_kernel(x_ref, w_hbm, cwg_ref, cwv_ref, cbg_ref, cbv_ref, o_ref, wb_ref, stage_ref, sems, halo_ref,
               *, seq, layer, dff):
    i = pl.program_id(1)
    tm, tn = o_ref.shape
    slot, finish = _stream_weights(w_hbm, wb_ref, stage_ref, sems, layer=layer,
                                   col_starts=lambda n: [n * tn, dff + n * tn])

    @pl.when(lax.rem(i * tm, seq) == 0)
    def _():
        halo_ref[...] = jnp.zeros_like(halo_ref)

    row = lax.broadcasted_iota(jnp.int32, (ROW_CHUNK, tn), 0)
    for c in range(tm // ROW_CHUNK):
        rows = slice(c * ROW_CHUNK, (c + 1) * ROW_CHUNK)
        h = jnp.dot(x_ref[rows, :], wb_ref[slot], preferred_element_type=F32)
        halo = halo_ref[...]
        halo_ref[...] = h[ROW_CHUNK - 8:, :]
        acts = []
        for half, (cw_ref, cb_ref) in enumerate(((cwg_ref, cbg_ref), (cwv_ref, cbv_ref))):
            hh = h[:, half * tn:(half + 1) * tn]
            hl = halo[:, half * tn:(half + 1) * tn]
            h1 = jnp.where(row == 0, hl[7:8], pltpu.roll(hh, 1, 0))
            h2 = jnp.where(row == 0, hl[6:7], jnp.where(row == 1, hl[7:8], pltpu.roll(hh, 2, 0)))
            cw = cw_ref[...]
            acts.append(cb_ref[...] + cw[0:1] * h2 + cw[1:2] * h1 + cw[2:3] * hh)
        gate, val = acts
        o_ref[rows, :] = (gate * jax.nn.sigmoid(gate) * val).astype(o_ref.dtype)
    finish()


def up_conv_glu(xb, w_up, conv_w, conv_b, layer, *, seq, tm, tn):
    m, k = xb.shape
    dff = w_up.shape[-1] // 2
    nb = dff // tn
    assert seq % tm == 0 and dff % tn == 0
    conv_b = conv_b.reshape(conv_b.shape[0], 1, 2 * dff)
    return pl.pallas_call(
        functools.partial(_up_kernel, seq=seq, layer=layer, dff=dff),
        grid=(nb, m // tm),
        in_specs=[
            pl.BlockSpec((tm, k), lambda n, i: (i, 0)),
            pl.BlockSpec(memory_space=pl.ANY),
            pl.BlockSpec((None, CONV_WIDTH, tn), lambda n, i: (layer, 0, n)),
            pl.BlockSpec((None, CONV_WIDTH, tn), lambda n, i: (layer, 0, nb + n)),
            pl.BlockSpec((None, 1, tn), lambda n, i: (layer, 0, n)),
            pl.BlockSpec((None, 1, tn), lambda n, i: (layer, 0, nb + n)),
        ],
        out_specs=pl.BlockSpec((tm, tn), lambda n, i: (i, n)),
        out_shape=jax.ShapeDtypeStruct((m, dff), BF16),
        scratch_shapes=_weight_scratch(k, 2 * tn, m // tm, pieces=2) + [pltpu.VMEM((8, 2 * tn), F32)],
        compiler_params=_params(2),
        name="up_conv_glu",
    )(xb, w_up, conv_w, conv_w, conv_b, conv_b)


class _Tiles:
    mm_rows = 1024
    qkv_cols = 768
    qkv_c_cols = 768
    gate_cols = 768
    branch_cols = 512
    out_cols = 512
    up_rows = 2048
    up_cols = 256
    down_rows = 512
    down_cols = 512
    attn_a_q = 1024
    attn_a_k = 512
    banded_tokens = 512
    strided_tokens = 4096
    ln_rows = 256
    cast_rows = 256


def _forward(x, w_in, diff_lambda, diff_norm_g, sink_logits, w_branch, w_o, ln1_g, ln1_b,
             w_up, conv_w, conv_b, w_down, ln2_g, ln2_b, tiles=_Tiles):
    batch, seq, d = x.shape
    depth = w_in.shape[0]
    tokens = batch * seq
    alpha = (2 * depth) ** 0.25

    slopes = jnp.exp2(-8.0 * (jnp.arange(N_ALIBI_HEADS, dtype=F32) + 1.0) / N_ALIBI_HEADS)
    slopes2 = slopes * LOG2E
    colscale = np.ones((1, QKV_WIDTH), np.float32)
    for off, wdt in ((OFF_AQ, A_WIDTH), (OFF_BQ, B_WIDTH), (OFF_CQ, C_WIDTH)):
        colscale[:, off:off + wdt] = SCALE * LOG2E
    colscale = jnp.asarray(colscale)

    x32 = x.reshape(tokens, d)
    xb = cast_bf16(x32, tiles.cast_rows)
    for l in range(depth):
        lam_init = 0.8 - 0.6 * math.exp(-0.3 * l)
        qkv = in_projection(xb, w_in, l, 0, OFF_CQ, colscale=colscale[:, :OFF_CQ], out_dtype=BF16,
                            tm=tiles.mm_rows, tn=tiles.qkv_cols)
        qkv_c = in_projection(xb, w_in, l, OFF_CQ, 3 * C_WIDTH, colscale=colscale[:, OFF_CQ:], out_dtype=F32,
                              tm=tiles.mm_rows, tn=tiles.qkv_c_cols)
        gates = in_projection(xb, w_in, l, QKV_WIDTH, 3 * d, colscale=None, out_dtype=BF16,
                              tm=tiles.mm_rows, tn=tiles.gate_cols)
        o_a = diff_attention(qkv, slopes2, diff_lambda[l], diff_norm_g[l], batch=batch, seq=seq,
                             lam_init=lam_init, tq=tiles.attn_a_q, tk=tiles.attn_a_k)
        (o_b,) = banded_attention(
            qkv, slopes2, sink_logits[l] * LOG2E, batch=batch, seq=seq, dil=1, tq=tiles.banded_tokens,
            hk=1, g=B_Q_HEADS // B_KV_HEADS, n_steps=B_KV_HEADS, q_off=OFF_BQ, k_off=OFF_BK, v_off=OFF_BV,
            max_dist=B_WINDOW - 1, slope0=0, out_width=B_WIDTH, out_dtype=BF16, want_lse=False)
        c_outs, c_lses = [], []
        gw = C_HEADS_PER_GROUP * HEAD_DIM
        for gi, (window, dil) in enumerate(C_GROUPS):
            hk = C_HEADS_PER_GROUP if dil == 1 else 1
            o, lse = banded_attention(
                qkv_c, slopes2, None, batch=batch, seq=seq, dil=dil,
                tq=tiles.banded_tokens if dil == 1 else max(tiles.strided_tokens // dil, BLOCK),
                hk=hk, g=1, n_steps=C_HEADS_PER_GROUP // hk,
                q_off=gi * gw, k_off=C_WIDTH + gi * gw, v_off=2 * C_WIDTH + gi * gw,
                max_dist=window // dil, slope0=B_Q_HEADS + gi * C_HEADS_PER_GROUP,
                out_width=gw, out_dtype=F32, want_lse=True)
            c_outs.append(o)
            c_lses.append(lse)
        o_c = merge_groups(c_outs, c_lses, tiles.ln_rows)
        y = branch_projection(o_a, o_b, o_c, gates, w_branch, l, tm=tiles.mm_rows, tn=tiles.branch_cols)
        pre = residual_matmul(y, w_o, l, x32, alpha=alpha, tm=tiles.mm_rows, tn=tiles.out_cols)
        x32, xb = layer_norm(pre, ln1_g[l], ln1_b[l], tm=tiles.ln_rows)
        act = up_conv_glu(xb, w_up, conv_w, conv_b, l, seq=seq, tm=tiles.up_rows, tn=tiles.up_cols)
        wdb = cast_bf16(w_down, tiles.cast_rows, layer=l)
        pre = residual_matmul_bf16(act, wdb, x32, alpha=alpha, tm=tiles.down_rows, tn=tiles.down_cols)
        x32, xb = layer_norm(pre, ln2_g[l], ln2_b[l], tm=tiles.ln_rows)
    return x32.reshape(batch, seq, d)


def kernel(x, w_in, diff_lambda, diff_norm_g, sink_logits, w_branch, w_o, ln1_g, ln1_b,
           w_up, conv_w, conv_b, w_down, ln2_g, ln2_b):
    return _forward(x, w_in, diff_lambda, diff_norm_g, sink_logits, w_branch, w_o, ln1_g, ln1_b,
                    w_up, conv_w, conv_b, w_down, ln2_g, ln2_b)
```

```python
import functools
import math

import numpy as np
import jax
import jax.numpy as jnp
from jax import lax
from jax.experimental import pallas as pl
from jax.experimental.pallas import tpu as pltpu

F32 = jnp.float32
BF16 = jnp.bfloat16

HEAD_DIM = 128
BLOCK = 128
SCALE = HEAD_DIM ** -0.5
A_HEADS = 6
A_V_DIM = 2 * HEAD_DIM
B_Q_HEADS = 8
B_KV_HEADS = 2
B_WINDOW = 128
C_GROUPS = ((128, 1), (512, 4), (2048, 16))
C_HEADS_PER_GROUP = 4
C_HEADS = C_HEADS_PER_GROUP * len(C_GROUPS)
N_ALIBI_HEADS = B_Q_HEADS + C_HEADS + A_HEADS
A_WIDTH = A_HEADS * A_V_DIM
B_WIDTH = B_Q_HEADS * HEAD_DIM
C_WIDTH = C_HEADS * HEAD_DIM
CONV_WIDTH = 3
LN_EPS = 1e-5
RMS_EPS = 1e-5
LOG2E = math.log2(math.e)
LN2 = math.log(2.0)
NEG_BIG = -1e30

OFF_AQ = 0
OFF_AK = OFF_AQ + A_WIDTH
OFF_AV = OFF_AK + A_WIDTH
OFF_BQ = OFF_AV + A_WIDTH
OFF_BK = OFF_BQ + B_WIDTH
OFF_BV = OFF_BK + B_KV_HEADS * HEAD_DIM
OFF_CQ = OFF_BV + B_KV_HEADS * HEAD_DIM
OFF_CK = OFF_CQ + C_WIDTH
OFF_CV = OFF_CK + C_WIDTH
QKV_WIDTH = OFF_CV + C_WIDTH

V7X_VMEM_LIMIT_BYTES = 60 * 1024 * 1024
ROW_CHUNK = 256


def _params(n_axes, vmem=V7X_VMEM_LIMIT_BYTES):
    return pltpu.CompilerParams(dimension_semantics=("arbitrary",) * n_axes, vmem_limit_bytes=vmem)


def _cast_kernel(x_ref, o_ref):
    o_ref[...] = x_ref[...].astype(o_ref.dtype)


def cast_bf16(x, rows):
    r, c = x.shape
    return pl.pallas_call(
        _cast_kernel,
        grid=(r // rows,),
        in_specs=[pl.BlockSpec((rows, c), lambda i: (i, 0))],
        out_specs=pl.BlockSpec((rows, c), lambda i: (i, 0)),
        out_shape=jax.ShapeDtypeStruct((r, c), BF16),
        compiler_params=_params(1),
        name="cast_bf16",
    )(x)


def _cast_weight(w_ref, wb_ref, col0=0):
    width = w_ref.shape[1]

    def body(i, c):
        r = pl.multiple_of(i * ROW_CHUNK, ROW_CHUNK)
        wb_ref[pl.ds(r, ROW_CHUNK), col0:col0 + width] = w_ref[pl.ds(r, ROW_CHUNK), :].astype(BF16)
        return c

    lax.fori_loop(0, w_ref.shape[0] // ROW_CHUNK, body, 0)


def _inproj_scale_kernel(x_ref, w_ref, cs_ref, o_ref, wb_ref):
    @pl.when(pl.program_id(1) == 0)
    def _():
        _cast_weight(w_ref, wb_ref)

    for c in range(x_ref.shape[0] // ROW_CHUNK):
        rows = slice(c * ROW_CHUNK, (c + 1) * ROW_CHUNK)
        acc = jnp.dot(x_ref[rows, :], wb_ref[...], preferred_element_type=F32)
        o_ref[rows, :] = (acc * cs_ref[...]).astype(o_ref.dtype)


def _inproj_sigmoid_kernel(x_ref, w_ref, o_ref, wb_ref):
    @pl.when(pl.program_id(1) == 0)
    def _():
        _cast_weight(w_ref, wb_ref)

    for c in range(x_ref.shape[0] // ROW_CHUNK):
        rows = slice(c * ROW_CHUNK, (c + 1) * ROW_CHUNK)
        acc = jnp.dot(x_ref[rows, :], wb_ref[...], preferred_element_type=F32)
        o_ref[rows, :] = jax.nn.sigmoid(acc).astype(o_ref.dtype)


def in_projection(xb, w, layer, col0, width, *, colscale, out_dtype, tm, tn):
    m, k = xb.shape
    assert col0 % tn == 0 and width % tn == 0 and m % tm == 0
    cb = col0 // tn
    in_specs = [
        pl.BlockSpec((tm, k), lambda n, i: (i, 0)),
        pl.BlockSpec((None, k, tn), lambda n, i: (layer, 0, cb + n)),
    ]
    args = [xb, w]
    if colscale is not None:
        in_specs.append(pl.BlockSpec((1, tn), lambda n, i: (0, n)))
        args.append(colscale)
    return pl.pallas_call(
        _inproj_sigmoid_kernel if colscale is None else _inproj_scale_kernel,
        grid=(width // tn, m // tm),
        in_specs=in_specs,
        out_specs=pl.BlockSpec((tm, tn), lambda n, i: (i, n)),
        out_shape=jax.ShapeDtypeStruct((m, width), out_dtype),
        scratch_shapes=[pltpu.VMEM((k, tn), BF16)],
        compiler_params=_params(2),
        name="inproj_gates" if colscale is None else "inproj_qkv",
    )(*args)


def _lane_tile(x, n):
    return jnp.concatenate([x] * n, axis=1)


def _attn_a_kernel(slopes_ref, q_ref, k_ref, v_ref, lp_ref, g_ref, o_ref, m_ref, l_ref, acc_ref,
                   *, tq, tk, lam_init, slope0):
    h = pl.program_id(1)
    qi = pl.program_id(2)
    slope = slopes_ref[slope0 + h]
    col = lax.broadcasted_iota(jnp.int32, (1, tk), 1)
    nt = (((1,), (1,)), ((), ()))

    m_ref[...] = jnp.full(m_ref.shape, NEG_BIG, F32)
    l_ref[...] = jnp.zeros(l_ref.shape, F32)
    acc_ref[...] = jnp.zeros(acc_ref.shape, F32)

    def step(r0, js, rel, masked):
        rows = slice(r0, tq)
        vj = v_ref[pl.ds(js, tk), :]
        cb = slope * (col + rel).astype(F32)
        for mi in range(2):
            hd = slice(mi * HEAD_DIM, (mi + 1) * HEAD_DIM)
            s = lax.dot_general(q_ref[rows, hd], k_ref[pl.ds(js, tk), hd], nt, preferred_element_type=F32) + cb
            if masked:
                keep = (lax.broadcasted_iota(jnp.int32, (tq - r0, tk), 0)
                        >= lax.broadcasted_iota(jnp.int32, (tq - r0, tk), 1))
                s = jnp.where(keep, s, NEG_BIG)
            mo = m_ref[mi, rows]
            mn = jnp.maximum(mo, jnp.max(s, axis=-1, keepdims=True))
            a = jnp.exp2(mo - mn)
            p = jnp.exp2(s - _lane_tile(mn, tk // HEAD_DIM))
            l_ref[mi, rows] = a * l_ref[mi, rows] + jnp.sum(p, axis=-1, keepdims=True)
            m_ref[mi, rows] = mn
            acc_ref[mi, rows] = (_lane_tile(a, A_V_DIM // HEAD_DIM) * acc_ref[mi, rows]
                                 + jnp.dot(p.astype(BF16), vj, preferred_element_type=F32))

    def body(j, c):
        js = pl.multiple_of(j * tk, tk)
        step(0, js, js - qi * tq, False)
        return c

    lax.fori_loop(0, qi * (tq // tk), body, 0)
    for jj in range(tq // tk):
        step(jj * tk, pl.multiple_of(qi * tq + jj * tk, tk), jj * tk, True)

    lp = lp_ref[...]
    lam = (jnp.exp(jnp.sum(lp[0:1] * lp[1:2], axis=-1, keepdims=True))
           - jnp.exp(jnp.sum(lp[2:3] * lp[3:4], axis=-1, keepdims=True)) + lam_init)
    rep = A_V_DIM // HEAD_DIM
    o = (acc_ref[0] * _lane_tile(1.0 / l_ref[0], rep)
         - lam * (acc_ref[1] * _lane_tile(1.0 / l_ref[1], rep)))
    o = o * lax.rsqrt(jnp.mean(jnp.square(o), axis=-1, keepdims=True) + RMS_EPS) * g_ref[...]
    o_ref[...] = (o * (1.0 - lam_init)).astype(o_ref.dtype)


def diff_attention(qkv, slopes2, lam_params, norm_g, *, batch, seq, lam_init, tq, tk):
    nq = seq // tq
    kb = OFF_AK // A_V_DIM
    vb = OFF_AV // A_V_DIM
    assert tq % tk == 0 and seq % tq == 0
    return pl.pallas_call(
        functools.partial(_attn_a_kernel, tq=tq, tk=tk, lam_init=lam_init, slope0=B_Q_HEADS + C_HEADS),
        grid=(batch, A_HEADS, nq),
        in_specs=[
            pl.BlockSpec(memory_space=pltpu.SMEM),
            pl.BlockSpec((tq, A_V_DIM), lambda b, h, i: (b * nq + i, h)),
            pl.BlockSpec((seq, A_V_DIM), lambda b, h, i: (b, kb + h)),
            pl.BlockSpec((seq, A_V_DIM), lambda b, h, i: (b, vb + h)),
            pl.BlockSpec((4, HEAD_DIM), lambda b, h, i: (0, 0)),
            pl.BlockSpec((1, A_V_DIM), lambda b, h, i: (0, 0)),
        ],
        out_specs=pl.BlockSpec((tq, A_V_DIM), lambda b, h, i: (b * nq + i, h)),
        out_shape=jax.ShapeDtypeStruct((batch * seq, A_WIDTH), BF16),
        scratch_shapes=[pltpu.VMEM((2, tq, HEAD_DIM), F32), pltpu.VMEM((2, tq, HEAD_DIM), F32),
                        pltpu.VMEM((2, tq, A_V_DIM), F32)],
        compiler_params=_params(3),
        name="diff_attention",
    )(slopes2, qkv, qkv, qkv, lam_params, norm_g.reshape(1, A_V_DIM))


def _banded_kernel(*refs, dil, tq, hk, g, max_dist, slope0, has_sink, want_lse):
    it = iter(refs)
    slopes_ref = next(it)
    sinks_ref = next(it) if has_sink else None
    q_ref, kp_ref, k_ref, vp_ref, v_ref = (next(it) for _ in range(5))
    o_ref = next(it)
    lse_ref = next(it) if want_lse else None

    khs = pl.program_id(1)
    qi = pl.program_id(2)
    nt = (((1,), (1,)), ((), ()))

    def rows_of(blk, r):
        if dil == 1:
            return slice(blk * BLOCK, (blk + 1) * BLOCK)
        return pl.ds(blk * BLOCK * dil + r, BLOCK, stride=dil)

    def residue(r):
        row = lax.broadcasted_iota(jnp.int32, (BLOCK, 2 * BLOCK), 0)
        col = lax.broadcasted_iota(jnp.int32, (BLOCK, 2 * BLOCK), 1)
        dist = row + BLOCK - col
        band = (dist >= 0) & (dist <= max_dist)
        first_key = jnp.where(qi > 0, 0, BLOCK)
        band_first = band & (col >= first_key)
        dist_f = (dist * dil).astype(F32)
        for sb in range(tq // BLOCK):
            rows = rows_of(sb, r)
            valid = band_first if sb == 0 else band
            for kh in range(hk):
                kc = slice(kh * HEAD_DIM, (kh + 1) * HEAD_DIM)
                if sb == 0:
                    kprev, vprev = kp_ref[rows_of(0, r), kc], vp_ref[rows_of(0, r), kc]
                else:
                    kprev, vprev = k_ref[rows_of(sb - 1, r), kc], v_ref[rows_of(sb - 1, r), kc]
                kk = jnp.concatenate([kprev, k_ref[rows, kc]], axis=0).astype(BF16)
                vv = jnp.concatenate([vprev, v_ref[rows, kc]], axis=0).astype(BF16)
                hc = [slice((kh * g + gi) * HEAD_DIM, (kh * g + gi + 1) * HEAD_DIM) for gi in range(g)]
                qs = jnp.concatenate([q_ref[rows, c] for c in hc], axis=0).astype(BF16)
                s_all = lax.dot_general(qs, kk, nt, preferred_element_type=F32)
                es, invs = [], []
                for gi in range(g):
                    head = (khs * hk + kh) * g + gi
                    slope = slopes_ref[slope0 + head]
                    s = s_all[gi * BLOCK:(gi + 1) * BLOCK] - slope * dist_f
                    s = jnp.where(valid, s, NEG_BIG)
                    m = jnp.max(s, axis=-1, keepdims=True)
                    if has_sink:
                        sk = sinks_ref[head]
                        m = jnp.maximum(m, sk)
                    e = jnp.exp2(s - m)
                    den = jnp.sum(e, axis=-1, keepdims=True)
                    if has_sink:
                        den = den + jnp.exp2(sk - m)
                    es.append(e.astype(BF16))
                    invs.append(1.0 / den)
                    if want_lse:
                        lse = (m + jnp.log2(den)) * LN2
                        lse_ref[rows, hc[gi]] = jnp.broadcast_to(lse, (BLOCK, HEAD_DIM))
                o_all = jnp.dot(jnp.concatenate(es, axis=0), vv, preferred_element_type=F32)
                for gi in range(g):
                    o = o_all[gi * BLOCK:(gi + 1) * BLOCK] * invs[gi]
                    o_ref[rows, hc[gi]] = o.astype(o_ref.dtype)

    if dil == 1:
        residue(0)
    else:
        def body(r, c):
            residue(r)
            return c

        lax.fori_loop(0, dil, body, 0)


def banded_attention(qkv, slopes2, sinks2, *, batch, seq, dil, tq, hk, g, n_steps, q_off, k_off, v_off,
                     max_dist, slope0, out_width, out_dtype, want_lse):
    tokens = qkv.shape[0]
    tt = tq * dil
    pt = BLOCK * dil
    nq = seq // tt
    qw = hk * g * HEAD_DIM
    kw = hk * HEAD_DIM
    qcb, kcb, vcb = q_off // qw, k_off // kw, v_off // kw
    assert q_off % qw == 0 and k_off % kw == 0 and v_off % kw == 0 and seq % tt == 0 and tt % pt == 0

    def prev_row(b, i):
        return jnp.maximum(b * (seq // pt) + i * (tt // pt) - 1, 0)

    has_sink = sinks2 is not None
    in_specs = [pl.BlockSpec(memory_space=pltpu.SMEM)]
    args = [slopes2]
    if has_sink:
        in_specs.append(pl.BlockSpec(memory_space=pltpu.SMEM))
        args.append(sinks2)
    in_specs += [
        pl.BlockSpec((tt, qw), lambda b, s, i: (b * nq + i, qcb + s)),
        pl.BlockSpec((pt, kw), lambda b, s, i: (prev_row(b, i), kcb + s)),
        pl.BlockSpec((tt, kw), lambda b, s, i: (b * nq + i, kcb + s)),
        pl.BlockSpec((pt, kw), lambda b, s, i: (prev_row(b, i), vcb + s)),
        pl.BlockSpec((tt, kw), lambda b, s, i: (b * nq + i, vcb + s)),
    ]
    args += [qkv] * 5
    out_spec = pl.BlockSpec((tt, qw), lambda b, s, i: (b * nq + i, s))
    out_shapes = [jax.ShapeDtypeStruct((tokens, out_width), out_dtype)]
    out_specs = [out_spec]
    if want_lse:
        out_shapes.append(jax.ShapeDtypeStruct((tokens, out_width), F32))
        out_specs.append(out_spec)
    return pl.pallas_call(
        functools.partial(_banded_kernel, dil=dil, tq=tq, hk=hk, g=g, max_dist=max_dist, slope0=slope0,
                          has_sink=has_sink, want_lse=want_lse),
        grid=(batch, n_steps, nq),
        in_specs=in_specs,
        out_specs=out_specs,
        out_shape=out_shapes,
        compiler_params=_params(3),
        name=f"banded_attention_d{dil}_g{g}",
    )(*args)


def _merge_kernel(o0_ref, o1_ref, o2_ref, l0_ref, l1_ref, l2_ref, out_ref):
    l0, l1, l2 = l0_ref[...], l1_ref[...], l2_ref[...]
    mx = jnp.maximum(jnp.maximum(l0, l1), l2)
    e0, e1, e2 = jnp.exp(l0 - mx), jnp.exp(l1 - mx), jnp.exp(l2 - mx)
    inv = 1.0 / (e0 + e1 + e2)
    gw = o0_ref.shape[1]
    out_ref[:, 0:gw] = (e0 * inv * o0_ref[...]).astype(out_ref.dtype)
    out_ref[:, gw:2 * gw] = (e1 * inv * o1_ref[...]).astype(out_ref.dtype)
    out_ref[:, 2 * gw:3 * gw] = (e2 * inv * o2_ref[...]).astype(out_ref.dtype)


def merge_groups(outs, lses, tm):
    tokens, gw = outs[0].shape
    spec = pl.BlockSpec((tm, gw), lambda i: (i, 0))
    return pl.pallas_call(
        _merge_kernel,
        grid=(tokens // tm,),
        in_specs=[spec] * 6,
        out_specs=pl.BlockSpec((tm, 3 * gw), lambda i: (i, 0)),
        out_shape=jax.ShapeDtypeStruct((tokens, 3 * gw), BF16),
        compiler_params=_params(1),
        name="merge_dilation_groups",
    )(*outs, *lses)


def _branch_kernel(oa_ref, ob_ref, oc_ref, ga_ref, gb_ref, gc_ref, w_ref, y_ref, wb_ref):
    @pl.when(pl.program_id(1) == 0)
    def _():
        _cast_weight(w_ref, wb_ref)

    tm = oa_ref.shape[0]
    ka, kb = oa_ref.shape[1], ob_ref.shape[1]
    for c in range(tm // ROW_CHUNK):
        rows = slice(c * ROW_CHUNK, (c + 1) * ROW_CHUNK)
        ya = jnp.dot(oa_ref[rows, :], wb_ref[0:ka, :], preferred_element_type=F32)
        yb = jnp.dot(ob_ref[rows, :], wb_ref[ka:ka + kb, :], preferred_element_type=F32)
        yc = jnp.dot(oc_ref[rows, :], wb_ref[ka + kb:, :], preferred_element_type=F32)
        y = (ga_ref[rows, :].astype(F32) * ya + gb_ref[rows, :].astype(F32) * yb
             + gc_ref[rows, :].astype(F32) * yc)
        y_ref[rows, :] = y.astype(y_ref.dtype)


def branch_projection(o_a, o_b, o_c, gates, w_branch, layer, *, tm, tn):
    m = o_a.shape[0]
    kmix, d = w_branch.shape[-2:]
    nb = d // tn
    return pl.pallas_call(
        _branch_kernel,
        grid=(nb, m // tm),
        in_specs=[
            pl.BlockSpec((tm, o_a.shape[1]), lambda n, i: (i, 0)),
            pl.BlockSpec((tm, o_b.shape[1]), lambda n, i: (i, 0)),
            pl.BlockSpec((tm, o_c.shape[1]), lambda n, i: (i, 0)),
            pl.BlockSpec((tm, tn), lambda n, i: (i, n)),
            pl.BlockSpec((tm, tn), lambda n, i: (i, nb + n)),
            pl.BlockSpec((tm, tn), lambda n, i: (i, 2 * nb + n)),
            pl.BlockSpec((None, kmix, tn), lambda n, i: (layer, 0, n)),
        ],
        out_specs=pl.BlockSpec((tm, tn), lambda n, i: (i, n)),
        out_shape=jax.ShapeDtypeStruct((m, d), BF16),
        scratch_shapes=[pltpu.VMEM((kmix, tn), BF16)],
        compiler_params=_params(2),
        name="branch_projection",
    )(o_a, o_b, o_c, gates, gates, gates, w_branch)


def _resid_mm_kernel(y_ref, w_ref, *refs, alpha):
    *resid, o_ref, wb_ref = refs

    @pl.when(pl.program_id(1) == 0)
    def _():
        _cast_weight(w_ref, wb_ref)

    for c in range(y_ref.shape[0] // ROW_CHUNK):
        rows = slice(c * ROW_CHUNK, (c + 1) * ROW_CHUNK)
        acc = jnp.dot(y_ref[rows, :], wb_ref[...], preferred_element_type=F32)
        o_ref[rows, :] = alpha * _residual_rows(resid, rows) + acc


def residual_matmul(y, w, layer, resid, *, alpha, tm, tn):
    m, k = y.shape
    d = w.shape[-1]
    return pl.pallas_call(
        functools.partial(_resid_mm_kernel, alpha=alpha),
        grid=(d // tn, m // tm),
        in_specs=[
            pl.BlockSpec((tm, k), lambda n, i: (i, 0)),
            pl.BlockSpec((None, k, tn), lambda n, i: (layer, 0, n)),
        ] + _residual_specs(resid, tm, tn, lambda n, i: i, lambda n, i: n),
        out_specs=pl.BlockSpec((tm, tn), lambda n, i: (i, n)),
        out_shape=jax.ShapeDtypeStruct((m, d), F32),
        scratch_shapes=[pltpu.VMEM((k, tn), BF16)],
        compiler_params=_params(2),
        name="out_projection",
    )(y, w, *resid)


def _resid_mm_bf16_kernel(a_ref, w_ref, *refs, alpha):
    *resid, o_ref = refs
    tm = a_ref.shape[0]
    for c in range(tm // ROW_CHUNK):
        rows = slice(c * ROW_CHUNK, (c + 1) * ROW_CHUNK)
        acc = jnp.dot(a_ref[rows, :], w_ref[...], preferred_element_type=F32)
        o_ref[rows, :] = alpha * _residual_rows(resid, rows) + acc


def residual_matmul_bf16(a, wb, resid, *, alpha, tm, tn):
    m, k = a.shape
    d = wb.shape[1]
    return pl.pallas_call(
        functools.partial(_resid_mm_bf16_kernel, alpha=alpha),
        grid=(m // tm, d // tn),
        in_specs=[
            pl.BlockSpec((tm, k), lambda i, n: (i, 0)),
            pl.BlockSpec((k, tn), lambda i, n: (0, n)),
        ] + _residual_specs(resid, tm, tn, lambda i, n: i, lambda i, n: n),
        out_specs=pl.BlockSpec((tm, tn), lambda i, n: (i, n)),
        out_shape=jax.ShapeDtypeStruct((m, d), F32),
        compiler_params=_params(2),
        name="down_projection",
    )(a, wb, *resid)


def _layer_norm_rows(x, mu, rstd, g, b):
    return (x - mu) * rstd * g + b


def _residual_rows(resid, rows):
    if len(resid) == 1:
        return resid[0][rows, :]
    pre_ref, mu_ref, rs_ref, g_ref, b_ref = resid
    n = pre_ref.shape[1] // mu_ref.shape[1]
    return _layer_norm_rows(pre_ref[rows, :], _lane_tile(mu_ref[rows, :], n), _lane_tile(rs_ref[rows, :], n),
                            g_ref[...], b_ref[...])


def _residual_specs(resid, tm, tn, row_of, col_of):
    tile = pl.BlockSpec((tm, tn), lambda *a: (row_of(*a), col_of(*a)))
    if len(resid) == 1:
        return [tile]
    stat = pl.BlockSpec((tm, HEAD_DIM), lambda *a: (row_of(*a), 0))
    vec = pl.BlockSpec((1, tn), lambda *a: (0, col_of(*a)))
    return [tile, stat, stat, vec, vec]


def _ln_stats_kernel(x_ref, g_ref, b_ref, ob_ref, mu_ref, rs_ref):
    x = x_ref[...]
    mu = jnp.mean(x, axis=-1, keepdims=True)
    rstd = lax.rsqrt(jnp.mean(jnp.square(x - mu), axis=-1, keepdims=True) + LN_EPS)
    ob_ref[...] = _layer_norm_rows(x, mu, rstd, g_ref[...], b_ref[...]).astype(ob_ref.dtype)
    mu_ref[...] = jnp.broadcast_to(mu, mu_ref.shape)
    rs_ref[...] = jnp.broadcast_to(rstd, rs_ref.shape)


def _ln_kernel(x_ref, g_ref, b_ref, o_ref):
    x = x_ref[...]
    mu = jnp.mean(x, axis=-1, keepdims=True)
    rstd = lax.rsqrt(jnp.mean(jnp.square(x - mu), axis=-1, keepdims=True) + LN_EPS)
    o_ref[...] = _layer_norm_rows(x, mu, rstd, g_ref[...], b_ref[...])


def layer_norm(x, g, b, *, tm, last):
    m, d = x.shape
    spec = pl.BlockSpec((tm, d), lambda i: (i, 0))
    vec = pl.BlockSpec((1, d), lambda i: (0, 0))
    stat = pl.BlockSpec((tm, HEAD_DIM), lambda i: (i, 0))
    g, b = g.reshape(1, d), b.reshape(1, d)
    if last:
        return pl.pallas_call(
            _ln_kernel, grid=(m // tm,), in_specs=[spec, vec, vec], out_specs=spec,
            out_shape=jax.ShapeDtypeStruct((m, d), F32), compiler_params=_params(1), name="layer_norm_out",
        )(x, g, b)
    xb, mu, rstd = pl.pallas_call(
        _ln_stats_kernel,
        grid=(m // tm,),
        in_specs=[spec, vec, vec],
        out_specs=[spec, stat, stat],
        out_shape=[jax.ShapeDtypeStruct((m, d), BF16), jax.ShapeDtypeStruct((m, HEAD_DIM), F32),
                   jax.ShapeDtypeStruct((m, HEAD_DIM), F32)],
        compiler_params=_params(1),
        name="layer_norm",
    )(x, g, b)
    return xb, (x, mu, rstd, g, b)


def _up_kernel(x_ref, wg_ref, wv_ref, cwg_ref, cwv_ref, cbg_ref, cbv_ref, wd_ref, o_ref, wdb_ref, wb_ref, halo_ref,
               *, seq):
    i = pl.program_id(1)
    tm, tn = o_ref.shape
    wdb_ref[...] = wd_ref[...].astype(wdb_ref.dtype)

    @pl.when(i == 0)
    def _():
        _cast_weight(wg_ref, wb_ref, 0)
        _cast_weight(wv_ref, wb_ref, tn)

    @pl.when(lax.rem(i * tm, seq) == 0)
    def _():
        halo_ref[...] = jnp.zeros_like(halo_ref)

    row = lax.broadcasted_iota(jnp.int32, (ROW_CHUNK, tn), 0)
    for c in range(tm // ROW_CHUNK):
        rows = slice(c * ROW_CHUNK, (c + 1) * ROW_CHUNK)
        h = jnp.dot(x_ref[rows, :], wb_ref[...], preferred_element_type=F32)
        halo = halo_ref[...]
        halo_ref[...] = h[ROW_CHUNK - 8:, :]
        acts = []
        for half, (cw_ref, cb_ref) in enumerate(((cwg_ref, cbg_ref), (cwv_ref, cbv_ref))):
            hh = h[:, half * tn:(half + 1) * tn]
            hl = halo[:, half * tn:(half + 1) * tn]
            h1 = jnp.where(row == 0, hl[7:8], pltpu.roll(hh, 1, 0))
            h2 = jnp.where(row == 0, hl[6:7], jnp.where(row == 1, hl[7:8], pltpu.roll(hh, 2, 0)))
            cw = cw_ref[...]
            acts.append(cb_ref[...] + cw[0:1] * h2 + cw[1:2] * h1 + cw[2:3] * hh)
        gate, val = acts
        o_ref[rows, :] = (gate * jax.nn.sigmoid(gate) * val).astype(o_ref.dtype)


def up_conv_glu(xb, w_up, conv_w, conv_b, w_down, layer, *, seq, tm, tn):
    m, k = xb.shape
    dff = w_up.shape[-1] // 2
    nb = dff // tn
    nm = m // tm
    assert seq % tm == 0 and dff % tn == 0
    kd, dd = w_down.shape[-2:]
    slab = kd // (nb * nm)
    assert slab * nb * nm == kd and slab % 16 == 0
    conv_b = conv_b.reshape(conv_b.shape[0], 1, 2 * dff)
    return pl.pallas_call(
        functools.partial(_up_kernel, seq=seq),
        grid=(nb, m // tm),
        in_specs=[
            pl.BlockSpec((tm, k), lambda n, i: (i, 0)),
            pl.BlockSpec((None, k, tn), lambda n, i: (layer, 0, n)),
            pl.BlockSpec((None, k, tn), lambda n, i: (layer, 0, nb + n)),
            pl.BlockSpec((None, CONV_WIDTH, tn), lambda n, i: (layer, 0, n)),
            pl.BlockSpec((None, CONV_WIDTH, tn), lambda n, i: (layer, 0, nb + n)),
            pl.BlockSpec((None, 1, tn), lambda n, i: (layer, 0, n)),
            pl.BlockSpec((None, 1, tn), lambda n, i: (layer, 0, nb + n)),
            pl.BlockSpec((None, slab, dd), lambda n, i: (layer, n * nm + i, 0)),
        ],
        out_specs=[pl.BlockSpec((tm, tn), lambda n, i: (i, n)),
                   pl.BlockSpec((slab, dd), lambda n, i: (n * nm + i, 0))],
        out_shape=[jax.ShapeDtypeStruct((m, dff), BF16), jax.ShapeDtypeStruct((kd, dd), BF16)],
        scratch_shapes=[pltpu.VMEM((k, 2 * tn), BF16), pltpu.VMEM((8, 2 * tn), F32)],
        compiler_params=_params(2),
        name="up_conv_glu",
    )(xb, w_up, w_up, conv_w, conv_w, conv_b, conv_b, w_down)


class _Tiles:
    mm_rows = 1024
    qkv_cols = 768
    qkv_c_cols = 768
    gate_cols = 768
    branch_cols = 512
    out_cols = 512
    up_rows = 2048
    up_cols = 256
    down_rows = 512
    down_cols = 512
    attn_a_q = 1024
    attn_a_k = 512
    banded_tokens = 512
    strided_tokens = 4096
    ln_rows = 256
    cast_rows = 256


def _forward(x, w_in, diff_lambda, diff_norm_g, sink_logits, w_branch, w_o, ln1_g, ln1_b,
             w_up, conv_w, conv_b, w_down, ln2_g, ln2_b, tiles=_Tiles):
    batch, seq, d = x.shape
    depth = w_in.shape[0]
    tokens = batch * seq
    alpha = (2 * depth) ** 0.25

    slopes = jnp.exp2(-8.0 * (jnp.arange(N_ALIBI_HEADS, dtype=F32) + 1.0) / N_ALIBI_HEADS)
    slopes2 = slopes * LOG2E
    colscale = np.ones((1, QKV_WIDTH), np.float32)
    for off, wdt in ((OFF_AQ, A_WIDTH), (OFF_BQ, B_WIDTH), (OFF_CQ, C_WIDTH)):
        colscale[:, off:off + wdt] = SCALE * LOG2E
    colscale = jnp.asarray(colscale)

    x32 = x.reshape(tokens, d)
    resid = (x32,)
    xb = cast_bf16(x32, tiles.cast_rows)
    for l in range(depth):
        lam_init = 0.8 - 0.6 * math.exp(-0.3 * l)
        qkv = in_projection(xb, w_in, l, 0, OFF_CQ, colscale=colscale[:, :OFF_CQ], out_dtype=BF16,
                            tm=tiles.mm_rows, tn=tiles.qkv_cols)
        qkv_c = in_projection(xb, w_in, l, OFF_CQ, 3 * C_WIDTH, colscale=colscale[:, OFF_CQ:], out_dtype=F32,
                              tm=tiles.mm_rows, tn=tiles.qkv_c_cols)
        gates = in_projection(xb, w_in, l, QKV_WIDTH, 3 * d, colscale=None, out_dtype=BF16,
                              tm=tiles.mm_rows, tn=tiles.gate_cols)
        o_a = diff_attention(qkv, slopes2, diff_lambda[l], diff_norm_g[l], batch=batch, seq=seq,
                             lam_init=lam_init, tq=tiles.attn_a_q, tk=tiles.attn_a_k)
        (o_b,) = banded_attention(
            qkv, slopes2, sink_logits[l] * LOG2E, batch=batch, seq=seq, dil=1, tq=tiles.banded_tokens,
            hk=1, g=B_Q_HEADS // B_KV_HEADS, n_steps=B_KV_HEADS, q_off=OFF_BQ, k_off=OFF_BK, v_off=OFF_BV,
            max_dist=B_WINDOW - 1, slope0=0, out_width=B_WIDTH, out_dtype=BF16, want_lse=False)
        c_outs, c_lses = [], []
        gw = C_HEADS_PER_GROUP * HEAD_DIM
        for gi, (window, dil) in enumerate(C_GROUPS):
            hk = C_HEADS_PER_GROUP if dil == 1 else 1
            o, lse = banded_attention(
                qkv_c, slopes2, None, batch=batch, seq=seq, dil=dil,
                tq=tiles.banded_tokens if dil == 1 else max(tiles.strided_tokens // dil, BLOCK),
                hk=hk, g=1, n_steps=C_HEADS_PER_GROUP // hk,
                q_off=gi * gw, k_off=C_WIDTH + gi * gw, v_off=2 * C_WIDTH + gi * gw,
                max_dist=window // dil, slope0=B_Q_HEADS + gi * C_HEADS_PER_GROUP,
                out_width=gw, out_dtype=F32, want_lse=True)
            c_outs.append(o)
            c_lses.append(lse)
        o_c = merge_groups(c_outs, c_lses, tiles.ln_rows)
        y = branch_projection(o_a, o_b, o_c, gates, w_branch, l, tm=tiles.mm_rows, tn=tiles.branch_cols)
        pre = residual_matmul(y, w_o, l, resid, alpha=alpha, tm=tiles.mm_rows, tn=tiles.out_cols)
        xb, resid = layer_norm(pre, ln1_g[l], ln1_b[l], tm=tiles.ln_rows, last=False)
        act, wdb = up_conv_glu(xb, w_up, conv_w, conv_b, w_down, l, seq=seq, tm=tiles.up_rows, tn=tiles.up_cols)
        pre = residual_matmul_bf16(act, wdb, resid, alpha=alpha, tm=tiles.down_rows, tn=tiles.down_cols)
        if l + 1 < depth:
            xb, resid = layer_norm(pre, ln2_g[l], ln2_b[l], tm=tiles.ln_rows, last=False)
    out = layer_norm(pre, ln2_g[depth - 1], ln2_b[depth - 1], tm=tiles.ln_rows, last=True)
    return out.reshape(batch, seq, d)


def kernel(x, w_in, diff_lambda, diff_norm_g, sink_logits, w_branch, w_o, ln1_g, ln1_b,
           w_up, conv_w, conv_b, w_down, ln2_g, ln2_b):
    return _forward(x, w_in, diff_lambda, diff_norm_g, sink_logits, w_branch, w_o, ln1_g, ln1_b,
                    w_up, conv_w, conv_b, w_down, ln2_g, ln2_b)
```

```python
import functools
import math

import numpy as np
import jax
import jax.numpy as jnp
from jax import lax
from jax.experimental import pallas as pl
from jax.experimental.pallas import tpu as pltpu

F32 = jnp.float32
BF16 = jnp.bfloat16

HEAD_DIM = 128
BLOCK = 128
SCALE = HEAD_DIM ** -0.5
A_HEADS = 6
A_V_DIM = 2 * HEAD_DIM
B_Q_HEADS = 8
B_KV_HEADS = 2
B_WINDOW = 128
C_GROUPS = ((128, 1), (512, 4), (2048, 16))
C_HEADS_PER_GROUP = 4
C_HEADS = C_HEADS_PER_GROUP * len(C_GROUPS)
N_ALIBI_HEADS = B_Q_HEADS + C_HEADS + A_HEADS
A_WIDTH = A_HEADS * A_V_DIM
B_WIDTH = B_Q_HEADS * HEAD_DIM
C_WIDTH = C_HEADS * HEAD_DIM
CONV_WIDTH = 3
LN_EPS = 1e-5
RMS_EPS = 1e-5
LOG2E = math.log2(math.e)
LN2 = math.log(2.0)
NEG_BIG = -1e30

OFF_AQ = 0
OFF_AK = OFF_AQ + A_WIDTH
OFF_AV = OFF_AK + A_WIDTH
OFF_BQ = OFF_AV + A_WIDTH
OFF_BK = OFF_BQ + B_WIDTH
OFF_BV = OFF_BK + B_KV_HEADS * HEAD_DIM
OFF_CQ = OFF_BV + B_KV_HEADS * HEAD_DIM
OFF_CK = OFF_CQ + C_WIDTH
OFF_CV = OFF_CK + C_WIDTH
QKV_WIDTH = OFF_CV + C_WIDTH

V7X_VMEM_LIMIT_BYTES = 60 * 1024 * 1024
ROW_CHUNK = 256


def _params(n_axes, vmem=V7X_VMEM_LIMIT_BYTES):
    return pltpu.CompilerParams(dimension_semantics=("arbitrary",) * n_axes, vmem_limit_bytes=vmem)


def _cast_kernel(x_ref, o_ref):
    o_ref[...] = x_ref[...].astype(o_ref.dtype)


def cast_bf16(x, rows):
    r, c = x.shape
    return pl.pallas_call(
        _cast_kernel,
        grid=(r // rows,),
        in_specs=[pl.BlockSpec((rows, c), lambda i: (i, 0))],
        out_specs=pl.BlockSpec((rows, c), lambda i: (i, 0)),
        out_shape=jax.ShapeDtypeStruct((r, c), BF16),
        compiler_params=_params(1),
        name="cast_bf16",
    )(x)


def _cast_weight(w_ref, wb_ref, col0=0):
    width = w_ref.shape[1]

    def body(i, c):
        r = pl.multiple_of(i * ROW_CHUNK, ROW_CHUNK)
        wb_ref[pl.ds(r, ROW_CHUNK), col0:col0 + width] = w_ref[pl.ds(r, ROW_CHUNK), :].astype(BF16)
        return c

    lax.fori_loop(0, w_ref.shape[0] // ROW_CHUNK, body, 0)


def _inproj_scale_kernel(x_ref, w_ref, cs_ref, o_ref, wb_ref):
    @pl.when(pl.program_id(1) == 0)
    def _():
        _cast_weight(w_ref, wb_ref)

    for c in range(x_ref.shape[0] // ROW_CHUNK):
        rows = slice(c * ROW_CHUNK, (c + 1) * ROW_CHUNK)
        acc = jnp.dot(x_ref[rows, :], wb_ref[...], preferred_element_type=F32)
        o_ref[rows, :] = (acc * cs_ref[...]).astype(o_ref.dtype)


def _inproj_sigmoid_kernel(x_ref, w_ref, *refs):
    n_side = (len(refs) - 2) // 2
    side_in, o_ref, side_out, wb_ref = refs[:n_side], refs[n_side], refs[n_side + 1:-1], refs[-1]
    for s_in, s_out in zip(side_in, side_out):
        s_out[...] = s_in[...].astype(s_out.dtype)

    @pl.when(pl.program_id(1) == 0)
    def _():
        _cast_weight(w_ref, wb_ref)

    for c in range(x_ref.shape[0] // ROW_CHUNK):
        rows = slice(c * ROW_CHUNK, (c + 1) * ROW_CHUNK)
        acc = jnp.dot(x_ref[rows, :], wb_ref[...], preferred_element_type=F32)
        o_ref[rows, :] = jax.nn.sigmoid(acc).astype(o_ref.dtype)


def in_projection(xb, w, layer, col0, width, *, colscale, out_dtype, tm, tn, side=()):
    m, k = xb.shape
    assert col0 % tn == 0 and width % tn == 0 and m % tm == 0
    cb = col0 // tn
    nm = m // tm
    steps = (width // tn) * nm
    in_specs = [
        pl.BlockSpec((tm, k), lambda n, i: (i, 0)),
        pl.BlockSpec((None, k, tn), lambda n, i: (layer, 0, cb + n)),
    ]
    args = [xb, w]
    out_specs = [pl.BlockSpec((tm, tn), lambda n, i: (i, n))]
    out_shapes = [jax.ShapeDtypeStruct((m, width), out_dtype)]
    if colscale is not None:
        assert not side
        in_specs.append(pl.BlockSpec((1, tn), lambda n, i: (0, n)))
        args.append(colscale)
    for arr in side:
        r, c = arr.shape[-2:]
        slab = r // steps
        assert slab * steps == r and slab % 16 == 0
        in_specs.append(pl.BlockSpec((None, slab, c), lambda n, i: (layer, n * nm + i, 0)))
        out_specs.append(pl.BlockSpec((slab, c), lambda n, i: (n * nm + i, 0)))
        out_shapes.append(jax.ShapeDtypeStruct((r, c), BF16))
        args.append(arr)
    res = pl.pallas_call(
        _inproj_sigmoid_kernel if colscale is None else _inproj_scale_kernel,
        grid=(width // tn, nm),
        in_specs=in_specs,
        out_specs=out_specs,
        out_shape=out_shapes,
        scratch_shapes=[pltpu.VMEM((k, tn), BF16)],
        compiler_params=_params(2),
        name="inproj_gates" if colscale is None else "inproj_qkv",
    )(*args)
    return res if side else res[0]


def _lane_tile(x, n):
    return jnp.concatenate([x] * n, axis=1)


def _attn_a_kernel(slopes_ref, q_ref, k_ref, v_ref, lp_ref, g_ref, o_ref, m_ref, l_ref, acc_ref,
                   *, tq, tk, lam_init, slope0):
    h = pl.program_id(1)
    qi = pl.program_id(2)
    slope = slopes_ref[slope0 + h]
    col = lax.broadcasted_iota(jnp.int32, (1, tk), 1)
    nt = (((1,), (1,)), ((), ()))

    m_ref[...] = jnp.full(m_ref.shape, NEG_BIG, F32)
    l_ref[...] = jnp.zeros(l_ref.shape, F32)
    acc_ref[...] = jnp.zeros(acc_ref.shape, F32)

    def step(r0, js, rel, masked):
        rows = slice(r0, tq)
        vj = v_ref[pl.ds(js, tk), :]
        cb = slope * (col + rel).astype(F32)
        for mi in range(2):
            hd = slice(mi * HEAD_DIM, (mi + 1) * HEAD_DIM)
            s = lax.dot_general(q_ref[rows, hd], k_ref[pl.ds(js, tk), hd], nt, preferred_element_type=F32) + cb
            if masked:
                keep = (lax.broadcasted_iota(jnp.int32, (tq - r0, tk), 0)
                        >= lax.broadcasted_iota(jnp.int32, (tq - r0, tk), 1))
                s = jnp.where(keep, s, NEG_BIG)
            mo = m_ref[mi, rows]
            mn = jnp.maximum(mo, jnp.max(s, axis=-1, keepdims=True))
            a = jnp.exp2(mo - mn)
            p = jnp.exp2(s - _lane_tile(mn, tk // HEAD_DIM))
            l_ref[mi, rows] = a * l_ref[mi, rows] + jnp.sum(p, axis=-1, keepdims=True)
            m_ref[mi, rows] = mn
            acc_ref[mi, rows] = (_lane_tile(a, A_V_DIM // HEAD_DIM) * acc_ref[mi, rows]
                                 + jnp.dot(p.astype(BF16), vj, preferred_element_type=F32))

    def body(j, c):
        js = pl.multiple_of(j * tk, tk)
        step(0, js, js - qi * tq, False)
        return c

    lax.fori_loop(0, qi * (tq // tk), body, 0)
    for jj in range(tq // tk):
        step(jj * tk, pl.multiple_of(qi * tq + jj * tk, tk), jj * tk, True)

    lp = lp_ref[...]
    lam = (jnp.exp(jnp.sum(lp[0:1] * lp[1:2], axis=-1, keepdims=True))
           - jnp.exp(jnp.sum(lp[2:3] * lp[3:4], axis=-1, keepdims=True)) + lam_init)
    rep = A_V_DIM // HEAD_DIM
    o = (acc_ref[0] * _lane_tile(1.0 / l_ref[0], rep)
         - lam * (acc_ref[1] * _lane_tile(1.0 / l_ref[1], rep)))
    o = o * lax.rsqrt(jnp.mean(jnp.square(o), axis=-1, keepdims=True) + RMS_EPS) * g_ref[...]
    o_ref[...] = (o * (1.0 - lam_init)).astype(o_ref.dtype)


def diff_attention(qkv, slopes2, lam_params, norm_g, *, batch, seq, lam_init, tq, tk):
    nq = seq // tq
    kb = OFF_AK // A_V_DIM
    vb = OFF_AV // A_V_DIM
    assert tq % tk == 0 and seq % tq == 0
    return pl.pallas_call(
        functools.partial(_attn_a_kernel, tq=tq, tk=tk, lam_init=lam_init, slope0=B_Q_HEADS + C_HEADS),
        grid=(batch, A_HEADS, nq),
        in_specs=[
            pl.BlockSpec(memory_space=pltpu.SMEM),
            pl.BlockSpec((tq, A_V_DIM), lambda b, h, i: (b * nq + i, h)),
            pl.BlockSpec((seq, A_V_DIM), lambda b, h, i: (b, kb + h)),
            pl.BlockSpec((seq, A_V_DIM), lambda b, h, i: (b, vb + h)),
            pl.BlockSpec((4, HEAD_DIM), lambda b, h, i: (0, 0)),
            pl.BlockSpec((1, A_V_DIM), lambda b, h, i: (0, 0)),
        ],
        out_specs=pl.BlockSpec((tq, A_V_DIM), lambda b, h, i: (b * nq + i, h)),
        out_shape=jax.ShapeDtypeStruct((batch * seq, A_WIDTH), BF16),
        scratch_shapes=[pltpu.VMEM((2, tq, HEAD_DIM), F32), pltpu.VMEM((2, tq, HEAD_DIM), F32),
                        pltpu.VMEM((2, tq, A_V_DIM), F32)],
        compiler_params=_params(3),
        name="diff_attention",
    )(slopes2, qkv, qkv, qkv, lam_params, norm_g.reshape(1, A_V_DIM))


def _banded_kernel(*refs, dil, tq, hk, g, max_dist, slope0, has_sink, want_lse, whole_seq):
    it = iter(refs)
    slopes_ref = next(it)
    sinks_ref = next(it) if has_sink else None
    q_ref, kp_ref, k_ref, vp_ref, v_ref = (next(it) for _ in range(5))
    o_ref = next(it)
    lse_ref = next(it) if want_lse else None

    khs = pl.program_id(1)
    qi = pl.program_id(2)
    nt = (((1,), (1,)), ((), ()))

    def rows_of(blk, r):
        if dil == 1:
            return slice(blk * BLOCK, (blk + 1) * BLOCK)
        return pl.ds(blk * BLOCK * dil + r, BLOCK, stride=dil)

    def residue(r):
        row = lax.broadcasted_iota(jnp.int32, (BLOCK, 2 * BLOCK), 0)
        col = lax.broadcasted_iota(jnp.int32, (BLOCK, 2 * BLOCK), 1)
        dist = row + BLOCK - col
        band = (dist >= 0) & (dist <= max_dist)
        first_key = jnp.where(qi > 0, 0, BLOCK)
        band_first = band & (col >= first_key)
        dist_f = (dist * dil).astype(F32)
        for sb in range(tq // BLOCK):
            rows = rows_of(sb, r)
            valid = band_first if sb == 0 else band
            for kh in range(hk):
                kc = slice(kh * HEAD_DIM, (kh + 1) * HEAD_DIM)
                if sb == 0 and whole_seq:
                    kprev = vprev = jnp.zeros((BLOCK, HEAD_DIM), k_ref.dtype)
                elif sb == 0:
                    kprev, vprev = kp_ref[rows_of(0, r), kc], vp_ref[rows_of(0, r), kc]
                else:
                    kprev, vprev = k_ref[rows_of(sb - 1, r), kc], v_ref[rows_of(sb - 1, r), kc]
                kk = jnp.concatenate([kprev, k_ref[rows, kc]], axis=0).astype(BF16)
                vv = jnp.concatenate([vprev, v_ref[rows, kc]], axis=0).astype(BF16)
                hc = [slice((kh * g + gi) * HEAD_DIM, (kh * g + gi + 1) * HEAD_DIM) for gi in range(g)]
                qs = jnp.concatenate([q_ref[rows, c] for c in hc], axis=0).astype(BF16)
                s_all = lax.dot_general(qs, kk, nt, preferred_element_type=F32)
                es, invs = [], []
                for gi in range(g):
                    head = (khs * hk + kh) * g + gi
                    slope = slopes_ref[slope0 + head]
                    s = s_all[gi * BLOCK:(gi + 1) * BLOCK] - slope * dist_f
                    s = jnp.where(valid, s, NEG_BIG)
                    m = jnp.max(s, axis=-1, keepdims=True)
                    if has_sink:
                        sk = sinks_ref[head]
                        m = jnp.maximum(m, sk)
                    e = jnp.exp2(s - m)
                    den = jnp.sum(e, axis=-1, keepdims=True)
                    if has_sink:
                        den = den + jnp.exp2(sk - m)
                    es.append(e.astype(BF16))
                    invs.append(1.0 / den)
                    if want_lse:
                        lse = (m + jnp.log2(den)) * LN2
                        lse_ref[rows, hc[gi]] = jnp.broadcast_to(lse, (BLOCK, HEAD_DIM))
                o_all = jnp.dot(jnp.concatenate(es, axis=0), vv, preferred_element_type=F32)
                for gi in range(g):
                    o = o_all[gi * BLOCK:(gi + 1) * BLOCK] * invs[gi]
                    o_ref[rows, hc[gi]] = o.astype(o_ref.dtype)

    if dil == 1:
        residue(0)
    else:
        def body(r, c):
            residue(r)
            return c

        lax.fori_loop(0, dil, body, 0)


def banded_attention(qkv, slopes2, sinks2, *, batch, seq, dil, tq, hk, g, n_steps, q_off, k_off, v_off,
                     max_dist, slope0, out_width, out_dtype, want_lse):
    tokens = qkv.shape[0]
    tt = tq * dil
    pt = BLOCK * dil
    nq = seq // tt
    qw = hk * g * HEAD_DIM
    kw = hk * HEAD_DIM
    qcb, kcb, vcb = q_off // qw, k_off // kw, v_off // kw
    assert q_off % qw == 0 and k_off % kw == 0 and v_off % kw == 0 and seq % tt == 0 and tt % pt == 0

    def prev_row(b, i):
        return jnp.maximum(b * (seq // pt) + i * (tt // pt) - 1, 0)

    has_sink = sinks2 is not None
    in_specs = [pl.BlockSpec(memory_space=pltpu.SMEM)]
    args = [slopes2]
    if has_sink:
        in_specs.append(pl.BlockSpec(memory_space=pltpu.SMEM))
        args.append(sinks2)
    in_specs += [
        pl.BlockSpec((tt, qw), lambda b, s, i: (b * nq + i, qcb + s)),
        pl.BlockSpec((pt, kw), lambda b, s, i: (prev_row(b, i), kcb + s)),
        pl.BlockSpec((tt, kw), lambda b, s, i: (b * nq + i, kcb + s)),
        pl.BlockSpec((pt, kw), lambda b, s, i: (prev_row(b, i), vcb + s)),
        pl.BlockSpec((tt, kw), lambda b, s, i: (b * nq + i, vcb + s)),
    ]
    args += [qkv] * 5
    out_spec = pl.BlockSpec((tt, qw), lambda b, s, i: (b * nq + i, s))
    out_shapes = [jax.ShapeDtypeStruct((tokens, out_width), out_dtype)]
    out_specs = [out_spec]
    if want_lse:
        out_shapes.append(jax.ShapeDtypeStruct((tokens, out_width), F32))
        out_specs.append(out_spec)
    return pl.pallas_call(
        functools.partial(_banded_kernel, dil=dil, tq=tq, hk=hk, g=g, max_dist=max_dist, slope0=slope0,
                          has_sink=has_sink, want_lse=want_lse, whole_seq=nq == 1),
        grid=(batch, n_steps, nq),
        in_specs=in_specs,
        out_specs=out_specs,
        out_shape=out_shapes,
        compiler_params=_params(3),
        name=f"banded_attention_d{dil}_g{g}",
    )(*args)


def _merge_kernel(o0_ref, o1_ref, o2_ref, l0_ref, l1_ref, l2_ref, out_ref):
    l0, l1, l2 = l0_ref[...], l1_ref[...], l2_ref[...]
    mx = jnp.maximum(jnp.maximum(l0, l1), l2)
    e0, e1, e2 = jnp.exp(l0 - mx), jnp.exp(l1 - mx), jnp.exp(l2 - mx)
    inv = 1.0 / (e0 + e1 + e2)
    gw = o0_ref.shape[1]
    out_ref[:, 0:gw] = (e0 * inv * o0_ref[...]).astype(out_ref.dtype)
    out_ref[:, gw:2 * gw] = (e1 * inv * o1_ref[...]).astype(out_ref.dtype)
    out_ref[:, 2 * gw:3 * gw] = (e2 * inv * o2_ref[...]).astype(out_ref.dtype)


def merge_groups(outs, lses, tm):
    tokens, gw = outs[0].shape
    spec = pl.BlockSpec((tm, gw), lambda i: (i, 0))
    return pl.pallas_call(
        _merge_kernel,
        grid=(tokens // tm,),
        in_specs=[spec] * 6,
        out_specs=pl.BlockSpec((tm, 3 * gw), lambda i: (i, 0)),
        out_shape=jax.ShapeDtypeStruct((tokens, 3 * gw), BF16),
        compiler_params=_params(1),
        name="merge_dilation_groups",
    )(*outs, *lses)


def _branch_kernel(oa_ref, ob_ref, oc_ref, ga_ref, gb_ref, gc_ref, w_ref, y_ref):
    tm = oa_ref.shape[0]
    ka, kb = oa_ref.shape[1], ob_ref.shape[1]
    for c in range(tm // ROW_CHUNK):
        rows = slice(c * ROW_CHUNK, (c + 1) * ROW_CHUNK)
        ya = jnp.dot(oa_ref[rows, :], w_ref[0:ka, :], preferred_element_type=F32)
        yb = jnp.dot(ob_ref[rows, :], w_ref[ka:ka + kb, :], preferred_element_type=F32)
        yc = jnp.dot(oc_ref[rows, :], w_ref[ka + kb:, :], preferred_element_type=F32)
        y = (ga_ref[rows, :].astype(F32) * ya + gb_ref[rows, :].astype(F32) * yb
             + gc_ref[rows, :].astype(F32) * yc)
        y_ref[rows, :] = y.astype(y_ref.dtype)


def branch_projection(o_a, o_b, o_c, gates, wb, *, tm, tn):
    m = o_a.shape[0]
    kmix, d = wb.shape
    nb = d // tn
    return pl.pallas_call(
        _branch_kernel,
        grid=(m // tm, nb),
        in_specs=[
            pl.BlockSpec((tm, o_a.shape[1]), lambda i, n: (i, 0)),
            pl.BlockSpec((tm, o_b.shape[1]), lambda i, n: (i, 0)),
            pl.BlockSpec((tm, o_c.shape[1]), lambda i, n: (i, 0)),
            pl.BlockSpec((tm, tn), lambda i, n: (i, n)),
            pl.BlockSpec((tm, tn), lambda i, n: (i, nb + n)),
            pl.BlockSpec((tm, tn), lambda i, n: (i, 2 * nb + n)),
            pl.BlockSpec((kmix, tn), lambda i, n: (0, n)),
        ],
        out_specs=pl.BlockSpec((tm, tn), lambda i, n: (i, n)),
        out_shape=jax.ShapeDtypeStruct((m, d), BF16),
        compiler_params=_params(2),
        name="branch_projection",
    )(o_a, o_b, o_c, gates, gates, gates, wb)


def _resid_mm_bf16_kernel(a_ref, w_ref, *refs, alpha):
    *resid, o_ref = refs
    tm = a_ref.shape[0]
    for c in range(tm // ROW_CHUNK):
        rows = slice(c * ROW_CHUNK, (c + 1) * ROW_CHUNK)
        acc = jnp.dot(a_ref[rows, :], w_ref[...], preferred_element_type=F32)
        o_ref[rows, :] = alpha * _residual_rows(resid, rows) + acc


def residual_matmul_bf16(a, wb, resid, *, alpha, tm, tn, name):
    m, k = a.shape
    d = wb.shape[1]
    return pl.pallas_call(
        functools.partial(_resid_mm_bf16_kernel, alpha=alpha),
        grid=(m // tm, d // tn),
        in_specs=[
            pl.BlockSpec((tm, k), lambda i, n: (i, 0)),
            pl.BlockSpec((k, tn), lambda i, n: (0, n)),
        ] + _residual_specs(resid, tm, tn, lambda i, n: i, lambda i, n: n),
        out_specs=pl.BlockSpec((tm, tn), lambda i, n: (i, n)),
        out_shape=jax.ShapeDtypeStruct((m, d), F32),
        compiler_params=_params(2),
        name=name,
    )(a, wb, *resid)


def _layer_norm_rows(x, mu, rstd, g, b):
    return (x - mu) * rstd * g + b


def _residual_rows(resid, rows):
    if len(resid) == 1:
        return resid[0][rows, :]
    pre_ref, mu_ref, rs_ref, g_ref, b_ref = resid
    n = pre_ref.shape[1] // mu_ref.shape[1]
    return _layer_norm_rows(pre_ref[rows, :], _lane_tile(mu_ref[rows, :], n), _lane_tile(rs_ref[rows, :], n),
                            g_ref[...], b_ref[...])


def _residual_specs(resid, tm, tn, row_of, col_of):
    tile = pl.BlockSpec((tm, tn), lambda *a: (row_of(*a), col_of(*a)))
    if len(resid) == 1:
        return [tile]
    stat = pl.BlockSpec((tm, HEAD_DIM), lambda *a: (row_of(*a), 0))
    vec = pl.BlockSpec((1, tn), lambda *a: (0, col_of(*a)))
    return [tile, stat, stat, vec, vec]


def _ln_stats_kernel(x_ref, g_ref, b_ref, ob_ref, mu_ref, rs_ref):
    x = x_ref[...]
    mu = jnp.mean(x, axis=-1, keepdims=True)
    rstd = lax.rsqrt(jnp.mean(jnp.square(x - mu), axis=-1, keepdims=True) + LN_EPS)
    ob_ref[...] = _layer_norm_rows(x, mu, rstd, g_ref[...], b_ref[...]).astype(ob_ref.dtype)
    mu_ref[...] = jnp.broadcast_to(mu, mu_ref.shape)
    rs_ref[...] = jnp.broadcast_to(rstd, rs_ref.shape)


def _ln_kernel(x_ref, g_ref, b_ref, o_ref):
    x = x_ref[...]
    mu = jnp.mean(x, axis=-1, keepdims=True)
    rstd = lax.rsqrt(jnp.mean(jnp.square(x - mu), axis=-1, keepdims=True) + LN_EPS)
    o_ref[...] = _layer_norm_rows(x, mu, rstd, g_ref[...], b_ref[...])


def layer_norm(x, g, b, *, tm, last):
    m, d = x.shape
    spec = pl.BlockSpec((tm, d), lambda i: (i, 0))
    vec = pl.BlockSpec((1, d), lambda i: (0, 0))
    stat = pl.BlockSpec((tm, HEAD_DIM), lambda i: (i, 0))
    g, b = g.reshape(1, d), b.reshape(1, d)
    if last:
        return pl.pallas_call(
            _ln_kernel, grid=(m // tm,), in_specs=[spec, vec, vec], out_specs=spec,
            out_shape=jax.ShapeDtypeStruct((m, d), F32), compiler_params=_params(1), name="layer_norm_out",
        )(x, g, b)
    xb, mu, rstd = pl.pallas_call(
        _ln_stats_kernel,
        grid=(m // tm,),
        in_specs=[spec, vec, vec],
        out_specs=[spec, stat, stat],
        out_shape=[jax.ShapeDtypeStruct((m, d), BF16), jax.ShapeDtypeStruct((m, HEAD_DIM), F32),
                   jax.ShapeDtypeStruct((m, HEAD_DIM), F32)],
        compiler_params=_params(1),
        name="layer_norm",
    )(x, g, b)
    return xb, (x, mu, rstd, g, b)


def _up_kernel(x_ref, wg_ref, wv_ref, cwg_ref, cwv_ref, cbg_ref, cbv_ref, wd_ref, o_ref, wdb_ref, wb_ref, halo_ref,
               *, seq):
    i = pl.program_id(1)
    tm, tn = o_ref.shape
    wdb_ref[...] = wd_ref[...].astype(wdb_ref.dtype)

    @pl.when(i == 0)
    def _():
        _cast_weight(wg_ref, wb_ref, 0)
        _cast_weight(wv_ref, wb_ref, tn)

    @pl.when(lax.rem(i * tm, seq) == 0)
    def _():
        halo_ref[...] = jnp.zeros_like(halo_ref)

    row = lax.broadcasted_iota(jnp.int32, (ROW_CHUNK, tn), 0)
    for c in range(tm // ROW_CHUNK):
        rows = slice(c * ROW_CHUNK, (c + 1) * ROW_CHUNK)
        h = jnp.dot(x_ref[rows, :], wb_ref[...], preferred_element_type=F32)
        halo = halo_ref[...]
        halo_ref[...] = h[ROW_CHUNK - 8:, :]
        acts = []
        for half, (cw_ref, cb_ref) in enumerate(((cwg_ref, cbg_ref), (cwv_ref, cbv_ref))):
            hh = h[:, half * tn:(half + 1) * tn]
            hl = halo[:, half * tn:(half + 1) * tn]
            h1 = jnp.where(row == 0, hl[7:8], pltpu.roll(hh, 1, 0))
            h2 = jnp.where(row == 0, hl[6:7], jnp.where(row == 1, hl[7:8], pltpu.roll(hh, 2, 0)))
            cw = cw_ref[...]
            acts.append(cb_ref[...] + cw[0:1] * h2 + cw[1:2] * h1 + cw[2:3] * hh)
        gate, val = acts
        o_ref[rows, :] = (gate * jax.nn.sigmoid(gate) * val).astype(o_ref.dtype)


def up_conv_glu(xb, w_up, conv_w, conv_b, w_down, layer, *, seq, tm, tn):
    m, k = xb.shape
    dff = w_up.shape[-1] // 2
    nb = dff // tn
    nm = m // tm
    assert seq % tm == 0 and dff % tn == 0
    kd, dd = w_down.shape[-2:]
    slab = kd // (nb * nm)
    assert slab * nb * nm == kd and slab % 16 == 0
    conv_b = conv_b.reshape(conv_b.shape[0], 1, 2 * dff)
    return pl.pallas_call(
        functools.partial(_up_kernel, seq=seq),
        grid=(nb, m // tm),
        in_specs=[
            pl.BlockSpec((tm, k), lambda n, i: (i, 0)),
            pl.BlockSpec((None, k, tn), lambda n, i: (layer, 0, n)),
            pl.BlockSpec((None, k, tn), lambda n, i: (layer, 0, nb + n)),
            pl.BlockSpec((None, CONV_WIDTH, tn), lambda n, i: (layer, 0, n)),
            pl.BlockSpec((None, CONV_WIDTH, tn), lambda n, i: (layer, 0, nb + n)),
            pl.BlockSpec((None, 1, tn), lambda n, i: (layer, 0, n)),
            pl.BlockSpec((None, 1, tn), lambda n, i: (layer, 0, nb + n)),
            pl.BlockSpec((None, slab, dd), lambda n, i: (layer, n * nm + i, 0)),
        ],
        out_specs=[pl.BlockSpec((tm, tn), lambda n, i: (i, n)),
                   pl.BlockSpec((slab, dd), lambda n, i: (n * nm + i, 0))],
        out_shape=[jax.ShapeDtypeStruct((m, dff), BF16), jax.ShapeDtypeStruct((kd, dd), BF16)],
        scratch_shapes=[pltpu.VMEM((k, 2 * tn), BF16), pltpu.VMEM((8, 2 * tn), F32)],
        compiler_params=_params(2),
        name="up_conv_glu",
    )(xb, w_up, w_up, conv_w, conv_w, conv_b, conv_b, w_down)


class _Tiles:
    mm_rows = 1024
    qkv_cols = 768
    qkv_c_cols = 768
    gate_cols = 768
    branch_cols = 1024
    out_cols = 1024
    up_rows = 2048
    up_cols = 256
    down_rows = 512
    down_cols = 512
    attn_a_q = 1024
    attn_a_k = 512
    banded_tokens = 512
    strided_tokens = 4096
    ln_rows = 256
    cast_rows = 256


def _forward(x, w_in, diff_lambda, diff_norm_g, sink_logits, w_branch, w_o, ln1_g, ln1_b,
             w_up, conv_w, conv_b, w_down, ln2_g, ln2_b, tiles=_Tiles):
    batch, seq, d = x.shape
    depth = w_in.shape[0]
    tokens = batch * seq
    alpha = (2 * depth) ** 0.25

    slopes = jnp.exp2(-8.0 * (jnp.arange(N_ALIBI_HEADS, dtype=F32) + 1.0) / N_ALIBI_HEADS)
    slopes2 = slopes * LOG2E
    colscale = np.ones((1, QKV_WIDTH), np.float32)
    for off, wdt in ((OFF_AQ, A_WIDTH), (OFF_BQ, B_WIDTH), (OFF_CQ, C_WIDTH)):
        colscale[:, off:off + wdt] = SCALE * LOG2E
    colscale = jnp.asarray(colscale)

    x32 = x.reshape(tokens, d)
    resid = (x32,)
    xb = cast_bf16(x32, tiles.cast_rows)
    for l in range(depth):
        lam_init = 0.8 - 0.6 * math.exp(-0.3 * l)
        qkv = in_projection(xb, w_in, l, 0, OFF_CQ, colscale=colscale[:, :OFF_CQ], out_dtype=BF16,
                            tm=tiles.mm_rows, tn=tiles.qkv_cols)
        qkv_c = in_projection(xb, w_in, l, OFF_CQ, 3 * C_WIDTH, colscale=colscale[:, OFF_CQ:], out_dtype=F32,
                              tm=tiles.mm_rows, tn=tiles.qkv_c_cols)
        gates, wbb, wob = in_projection(xb, w_in, l, QKV_WIDTH, 3 * d, colscale=None, out_dtype=BF16,
                                        tm=tiles.mm_rows, tn=tiles.gate_cols, side=(w_branch, w_o))
        o_a = diff_attention(qkv, slopes2, diff_lambda[l], diff_norm_g[l], batch=batch, seq=seq,
                             lam_init=lam_init, tq=tiles.attn_a_q, tk=tiles.attn_a_k)
        (o_b,) = banded_attention(
            qkv, slopes2, sink_logits[l] * LOG2E, batch=batch, seq=seq, dil=1, tq=tiles.banded_tokens,
            hk=1, g=B_Q_HEADS // B_KV_HEADS, n_steps=B_KV_HEADS, q_off=OFF_BQ, k_off=OFF_BK, v_off=OFF_BV,
            max_dist=B_WINDOW - 1, slope0=0, out_width=B_WIDTH, out_dtype=BF16, want_lse=False)
        c_outs, c_lses = [], []
        gw = C_HEADS_PER_GROUP * HEAD_DIM
        for gi, (window, dil) in enumerate(C_GROUPS):
            hk = C_HEADS_PER_GROUP if dil == 1 else 1
            o, lse = banded_attention(
                qkv_c, slopes2, None, batch=batch, seq=seq, dil=dil,
                tq=tiles.banded_tokens if dil == 1 else max(tiles.strided_tokens // dil, BLOCK),
                hk=hk, g=1, n_steps=C_HEADS_PER_GROUP // hk,
                q_off=gi * gw, k_off=C_WIDTH + gi * gw, v_off=2 * C_WIDTH + gi * gw,
                max_dist=window // dil, slope0=B_Q_HEADS + gi * C_HEADS_PER_GROUP,
                out_width=gw, out_dtype=F32, want_lse=True)
            c_outs.append(o)
            c_lses.append(lse)
        o_c = merge_groups(c_outs, c_lses, tiles.ln_rows)
        y = branch_projection(o_a, o_b, o_c, gates, wbb, tm=tiles.mm_rows, tn=tiles.branch_cols)
        pre = residual_matmul_bf16(y, wob, resid, alpha=alpha, tm=tiles.mm_rows, tn=tiles.out_cols,
                                   name="out_projection")
        xb, resid = layer_norm(pre, ln1_g[l], ln1_b[l], tm=tiles.ln_rows, last=False)
        act, wdb = up_conv_glu(xb, w_up, conv_w, conv_b, w_down, l, seq=seq, tm=tiles.up_rows, tn=tiles.up_cols)
        pre = residual_matmul_bf16(act, wdb, resid, alpha=alpha, tm=tiles.down_rows, tn=tiles.down_cols,
                                   name="down_projection")
        if l + 1 < depth:
            xb, resid = layer_norm(pre, ln2_g[l], ln2_b[l], tm=tiles.ln_rows, last=False)
    out = layer_norm(pre, ln2_g[depth - 1], ln2_b[depth - 1], tm=tiles.ln_rows, last=True)
    return out.reshape(batch, seq, d)


def kernel(x, w_in, diff_lambda, diff_norm_g, sink_logits, w_branch, w_o, ln1_g, ln1_b,
           w_up, conv_w, conv_b, w_down, ln2_g, ln2_b):
    return _forward(x, w_in, diff_lambda, diff_norm_g, sink_logits, w_branch, w_o, ln1_g, ln1_b,
                    w_up, conv_w, conv_b, w_down, ln2_g, ln2_b)
```

```python
import functools
import math

import numpy as np
import jax
import jax.numpy as jnp
from jax import lax
from jax.experimental import pallas as pl
from jax.experimental.pallas import tpu as pltpu

F32 = jnp.float32
BF16 = jnp.bfloat16

HEAD_DIM = 128
BLOCK = 128
SCALE = HEAD_DIM ** -0.5
A_HEADS = 6
A_V_DIM = 2 * HEAD_DIM
B_Q_HEADS = 8
B_KV_HEADS = 2
B_WINDOW = 128
C_GROUPS = ((128, 1), (512, 4), (2048, 16))
C_HEADS_PER_GROUP = 4
C_HEADS = C_HEADS_PER_GROUP * len(C_GROUPS)
N_ALIBI_HEADS = B_Q_HEADS + C_HEADS + A_HEADS
A_WIDTH = A_HEADS * A_V_DIM
B_WIDTH = B_Q_HEADS * HEAD_DIM
C_WIDTH = C_HEADS * HEAD_DIM
CONV_WIDTH = 3
LN_EPS = 1e-5
RMS_EPS = 1e-5
LOG2E = math.log2(math.e)
LN2 = math.log(2.0)
NEG_BIG = -1e30

OFF_AQ = 0
OFF_AK = OFF_AQ + A_WIDTH
OFF_AV = OFF_AK + A_WIDTH
OFF_BQ = OFF_AV + A_WIDTH
OFF_BK = OFF_BQ + B_WIDTH
OFF_BV = OFF_BK + B_KV_HEADS * HEAD_DIM
OFF_CQ = OFF_BV + B_KV_HEADS * HEAD_DIM
OFF_CK = OFF_CQ + C_WIDTH
OFF_CV = OFF_CK + C_WIDTH
QKV_WIDTH = OFF_CV + C_WIDTH

V7X_VMEM_LIMIT_BYTES = 60 * 1024 * 1024
ROW_CHUNK = 256


def _params(n_axes, vmem=V7X_VMEM_LIMIT_BYTES):
    return pltpu.CompilerParams(dimension_semantics=("arbitrary",) * n_axes, vmem_limit_bytes=vmem)


def _cast_kernel(x_ref, o_ref):
    o_ref[...] = x_ref[...].astype(o_ref.dtype)


def cast_bf16(x, rows):
    r, c = x.shape
    return pl.pallas_call(
        _cast_kernel,
        grid=(r // rows,),
        in_specs=[pl.BlockSpec((rows, c), lambda i: (i, 0))],
        out_specs=pl.BlockSpec((rows, c), lambda i: (i, 0)),
        out_shape=jax.ShapeDtypeStruct((r, c), BF16),
        compiler_params=_params(1),
        name="cast_bf16",
    )(x)


def _cast_weight(w_ref, wb_ref, col0=0):
    width = w_ref.shape[1]

    def body(i, c):
        r = pl.multiple_of(i * ROW_CHUNK, ROW_CHUNK)
        wb_ref[pl.ds(r, ROW_CHUNK), col0:col0 + width] = w_ref[pl.ds(r, ROW_CHUNK), :].astype(BF16)
        return c

    lax.fori_loop(0, w_ref.shape[0] // ROW_CHUNK, body, 0)


def _inproj_kernel(x_ref, w_ref, *refs, has_scale, pair_widths):
    n_side = len(pair_widths)
    cs_ref = refs[0] if has_scale else None
    refs = refs[1:] if has_scale else refs
    side_in, o_ref, side_out, wb_ref = refs[:n_side], refs[n_side], refs[n_side + 1:-1], refs[-1]
    for s_in, s_out, pw in zip(side_in, side_out, pair_widths):
        if pw is None:
            s_out[...] = s_in[...].astype(s_out.dtype)
        else:
            half = s_in.shape[1] // 2
            for j in range(half // pw):
                s_out[:, 2 * j * pw:(2 * j + 1) * pw] = s_in[:, j * pw:(j + 1) * pw].astype(s_out.dtype)
                s_out[:, (2 * j + 1) * pw:(2 * j + 2) * pw] = (
                    s_in[:, half + j * pw:half + (j + 1) * pw].astype(s_out.dtype))

    @pl.when(pl.program_id(1) == 0)
    def _():
        _cast_weight(w_ref, wb_ref)

    for c in range(x_ref.shape[0] // ROW_CHUNK):
        rows = slice(c * ROW_CHUNK, (c + 1) * ROW_CHUNK)
        acc = jnp.dot(x_ref[rows, :], wb_ref[...], preferred_element_type=F32)
        acc = acc * cs_ref[...] if has_scale else jax.nn.sigmoid(acc)
        o_ref[rows, :] = acc.astype(o_ref.dtype)


def in_projection(xb, w, layer, col0, width, *, colscale, out_dtype, tm, tn, side=(), name):
    m, k = xb.shape
    assert col0 % tn == 0 and width % tn == 0 and m % tm == 0
    cb = col0 // tn
    nm = m // tm
    steps = (width // tn) * nm
    in_specs = [
        pl.BlockSpec((tm, k), lambda n, i: (i, 0)),
        pl.BlockSpec((None, k, tn), lambda n, i: (layer, 0, cb + n)),
    ]
    args = [xb, w]
    out_specs = [pl.BlockSpec((tm, tn), lambda n, i: (i, n))]
    out_shapes = [jax.ShapeDtypeStruct((m, width), out_dtype)]
    if colscale is not None:
        in_specs.append(pl.BlockSpec((1, tn), lambda n, i: (0, n)))
        args.append(colscale)
    for arr, _ in side:
        r, c = arr.shape[-2:]
        slab = r // steps
        assert slab * steps == r and slab % 16 == 0
        in_specs.append(pl.BlockSpec((None, slab, c), lambda n, i: (layer, n * nm + i, 0)))
        out_specs.append(pl.BlockSpec((slab, c), lambda n, i: (n * nm + i, 0)))
        out_shapes.append(jax.ShapeDtypeStruct((r, c), BF16))
        args.append(arr)
    res = pl.pallas_call(
        functools.partial(_inproj_kernel, has_scale=colscale is not None,
                          pair_widths=tuple(pw for _, pw in side)),
        grid=(width // tn, nm),
        in_specs=in_specs,
        out_specs=out_specs,
        out_shape=out_shapes,
        scratch_shapes=[pltpu.VMEM((k, tn), BF16)],
        compiler_params=_params(2),
        name=name,
    )(*args)
    return res if side else res[0]


def _lane_tile(x, n):
    return jnp.concatenate([x] * n, axis=1)


def _attn_a_kernel(slopes_ref, q_ref, k_ref, v_ref, lp_ref, g_ref, o_ref, m_ref, l_ref, acc_ref,
                   *, tq, tk, lam_init, slope0):
    h = pl.program_id(1)
    qi = pl.program_id(2)
    slope = slopes_ref[slope0 + h]
    col = lax.broadcasted_iota(jnp.int32, (1, tk), 1)
    nt = (((1,), (1,)), ((), ()))

    m_ref[...] = jnp.full(m_ref.shape, NEG_BIG, F32)
    l_ref[...] = jnp.zeros(l_ref.shape, F32)
    acc_ref[...] = jnp.zeros(acc_ref.shape, F32)

    def step(r0, js, rel, masked):
        rows = slice(r0, tq)
        vj = v_ref[pl.ds(js, tk), :]
        cb = slope * (col + rel).astype(F32)
        for mi in range(2):
            hd = slice(mi * HEAD_DIM, (mi + 1) * HEAD_DIM)
            s = lax.dot_general(q_ref[rows, hd], k_ref[pl.ds(js, tk), hd], nt, preferred_element_type=F32) + cb
            if masked:
                keep = (lax.broadcasted_iota(jnp.int32, (tq - r0, tk), 0)
                        >= lax.broadcasted_iota(jnp.int32, (tq - r0, tk), 1))
                s = jnp.where(keep, s, NEG_BIG)
            mo = m_ref[mi, rows]
            mn = jnp.maximum(mo, jnp.max(s, axis=-1, keepdims=True))
            a = jnp.exp2(mo - mn)
            p = jnp.exp2(s - _lane_tile(mn, tk // HEAD_DIM))
            l_ref[mi, rows] = a * l_ref[mi, rows] + jnp.sum(p, axis=-1, keepdims=True)
            m_ref[mi, rows] = mn
            acc_ref[mi, rows] = (_lane_tile(a, A_V_DIM // HEAD_DIM) * acc_ref[mi, rows]
                                 + jnp.dot(p.astype(BF16), vj, preferred_element_type=F32))

    def body(j, c):
        js = pl.multiple_of(j * tk, tk)
        step(0, js, js - qi * tq, False)
        return c

    lax.fori_loop(0, qi * (tq // tk), body, 0)
    for jj in range(tq // tk):
        step(jj * tk, pl.multiple_of(qi * tq + jj * tk, tk), jj * tk, True)

    lp = lp_ref[...]
    lam = (jnp.exp(jnp.sum(lp[0:1] * lp[1:2], axis=-1, keepdims=True))
           - jnp.exp(jnp.sum(lp[2:3] * lp[3:4], axis=-1, keepdims=True)) + lam_init)
    rep = A_V_DIM // HEAD_DIM
    o = (acc_ref[0] * _lane_tile(1.0 / l_ref[0], rep)
         - lam * (acc_ref[1] * _lane_tile(1.0 / l_ref[1], rep)))
    o = o * lax.rsqrt(jnp.mean(jnp.square(o), axis=-1, keepdims=True) + RMS_EPS) * g_ref[...]
    o_ref[...] = (o * (1.0 - lam_init)).astype(o_ref.dtype)


def diff_attention(qkv, slopes2, lam_params, norm_g, *, batch, seq, lam_init, tq, tk):
    nq = seq // tq
    kb = OFF_AK // A_V_DIM
    vb = OFF_AV // A_V_DIM
    assert tq % tk == 0 and seq % tq == 0
    return pl.pallas_call(
        functools.partial(_attn_a_kernel, tq=tq, tk=tk, lam_init=lam_init, slope0=B_Q_HEADS + C_HEADS),
        grid=(batch, A_HEADS, nq),
        in_specs=[
            pl.BlockSpec(memory_space=pltpu.SMEM),
            pl.BlockSpec((tq, A_V_DIM), lambda b, h, i: (b * nq + i, h)),
            pl.BlockSpec((seq, A_V_DIM), lambda b, h, i: (b, kb + h)),
            pl.BlockSpec((seq, A_V_DIM), lambda b, h, i: (b, vb + h)),
            pl.BlockSpec((4, HEAD_DIM), lambda b, h, i: (0, 0)),
            pl.BlockSpec((1, A_V_DIM), lambda b, h, i: (0, 0)),
        ],
        out_specs=pl.BlockSpec((tq, A_V_DIM), lambda b, h, i: (b * nq + i, h)),
        out_shape=jax.ShapeDtypeStruct((batch * seq, A_WIDTH), BF16),
        scratch_shapes=[pltpu.VMEM((2, tq, HEAD_DIM), F32), pltpu.VMEM((2, tq, HEAD_DIM), F32),
                        pltpu.VMEM((2, tq, A_V_DIM), F32)],
        compiler_params=_params(3),
        name="diff_attention",
    )(slopes2, qkv, qkv, qkv, lam_params, norm_g.reshape(1, A_V_DIM))


def _banded_kernel(*refs, dil, tq, hk, g, max_dist, slope0, has_sink, want_lse, whole_seq):
    it = iter(refs)
    slopes_ref = next(it)
    sinks_ref = next(it) if has_sink else None
    q_ref, kp_ref, k_ref, vp_ref, v_ref = (next(it) for _ in range(5))
    o_ref = next(it)
    lse_ref = next(it) if want_lse else None

    khs = pl.program_id(1)
    qi = pl.program_id(2)
    nt = (((1,), (1,)), ((), ()))

    def rows_of(blk, r):
        if dil == 1:
            return slice(blk * BLOCK, (blk + 1) * BLOCK)
        return pl.ds(blk * BLOCK * dil + r, BLOCK, stride=dil)

    def residue(r):
        row = lax.broadcasted_iota(jnp.int32, (BLOCK, 2 * BLOCK), 0)
        col = lax.broadcasted_iota(jnp.int32, (BLOCK, 2 * BLOCK), 1)
        dist = row + BLOCK - col
        band = (dist >= 0) & (dist <= max_dist)
        first_key = jnp.where(qi > 0, 0, BLOCK)
        band_first = band & (col >= first_key)
        dist_f = (dist * dil).astype(F32)
        for sb in range(tq // BLOCK):
            rows = rows_of(sb, r)
            valid = band_first if sb == 0 else band
            for kh in range(hk):
                kc = slice(kh * HEAD_DIM, (kh + 1) * HEAD_DIM)
                if sb == 0 and whole_seq:
                    kprev = vprev = jnp.zeros((BLOCK, HEAD_DIM), k_ref.dtype)
                elif sb == 0:
                    kprev, vprev = kp_ref[rows_of(0, r), kc], vp_ref[rows_of(0, r), kc]
                else:
                    kprev, vprev = k_ref[rows_of(sb - 1, r), kc], v_ref[rows_of(sb - 1, r), kc]
                kk = jnp.concatenate([kprev, k_ref[rows, kc]], axis=0).astype(BF16)
                vv = jnp.concatenate([vprev, v_ref[rows, kc]], axis=0).astype(BF16)
                hc = [slice((kh * g + gi) * HEAD_DIM, (kh * g + gi + 1) * HEAD_DIM) for gi in range(g)]
                qs = jnp.concatenate([q_ref[rows, c] for c in hc], axis=0).astype(BF16)
                s_all = lax.dot_general(qs, kk, nt, preferred_element_type=F32)
                es, invs = [], []
                for gi in range(g):
                    head = (khs * hk + kh) * g + gi
                    slope = slopes_ref[slope0 + head]
                    s = s_all[gi * BLOCK:(gi + 1) * BLOCK] - slope * dist_f
                    s = jnp.where(valid, s, NEG_BIG)
                    m = jnp.max(s, axis=-1, keepdims=True)
                    if has_sink:
                        sk = sinks_ref[head]
                        m = jnp.maximum(m, sk)
                    e = jnp.exp2(s - m)
                    den = jnp.sum(e, axis=-1, keepdims=True)
                    if has_sink:
                        den = den + jnp.exp2(sk - m)
                    es.append(e.astype(BF16))
                    invs.append(1.0 / den)
                    if want_lse:
                        lse = (m + jnp.log2(den)) * LN2
                        lse_ref[rows, hc[gi]] = jnp.broadcast_to(lse, (BLOCK, HEAD_DIM))
                o_all = jnp.dot(jnp.concatenate(es, axis=0), vv, preferred_element_type=F32)
                for gi in range(g):
                    o = o_all[gi * BLOCK:(gi + 1) * BLOCK] * invs[gi]
                    o_ref[rows, hc[gi]] = o.astype(o_ref.dtype)

    if dil == 1:
        residue(0)
    else:
        def body(r, c):
            residue(r)
            return c

        lax.fori_loop(0, dil, body, 0)


def banded_attention(qkv, slopes2, sinks2, *, batch, seq, dil, tq, hk, g, n_steps, q_off, k_off, v_off,
                     max_dist, slope0, out_width, out_dtype, want_lse):
    tokens = qkv.shape[0]
    tt = tq * dil
    pt = BLOCK * dil
    nq = seq // tt
    qw = hk * g * HEAD_DIM
    kw = hk * HEAD_DIM
    qcb, kcb, vcb = q_off // qw, k_off // kw, v_off // kw
    assert q_off % qw == 0 and k_off % kw == 0 and v_off % kw == 0 and seq % tt == 0 and tt % pt == 0

    def prev_row(b, i):
        return jnp.maximum(b * (seq // pt) + i * (tt // pt) - 1, 0)

    has_sink = sinks2 is not None
    in_specs = [pl.BlockSpec(memory_space=pltpu.SMEM)]
    args = [slopes2]
    if has_sink:
        in_specs.append(pl.BlockSpec(memory_space=pltpu.SMEM))
        args.append(sinks2)
    in_specs += [
        pl.BlockSpec((tt, qw), lambda b, s, i: (b * nq + i, qcb + s)),
        pl.BlockSpec((pt, kw), lambda b, s, i: (prev_row(b, i), kcb + s)),
        pl.BlockSpec((tt, kw), lambda b, s, i: (b * nq + i, kcb + s)),
        pl.BlockSpec((pt, kw), lambda b, s, i: (prev_row(b, i), vcb + s)),
        pl.BlockSpec((tt, kw), lambda b, s, i: (b * nq + i, vcb + s)),
    ]
    args += [qkv] * 5
    out_spec = pl.BlockSpec((tt, qw), lambda b, s, i: (b * nq + i, s))
    out_shapes = [jax.ShapeDtypeStruct((tokens, out_width), out_dtype)]
    out_specs = [out_spec]
    if want_lse:
        out_shapes.append(jax.ShapeDtypeStruct((tokens, out_width), F32))
        out_specs.append(out_spec)
    return pl.pallas_call(
        functools.partial(_banded_kernel, dil=dil, tq=tq, hk=hk, g=g, max_dist=max_dist, slope0=slope0,
                          has_sink=has_sink, want_lse=want_lse, whole_seq=nq == 1),
        grid=(batch, n_steps, nq),
        in_specs=in_specs,
        out_specs=out_specs,
        out_shape=out_shapes,
        compiler_params=_params(3),
        name=f"banded_attention_d{dil}_g{g}",
    )(*args)


def _merge_kernel(o0_ref, o1_ref, o2_ref, l0_ref, l1_ref, l2_ref, out_ref):
    l0, l1, l2 = l0_ref[...], l1_ref[...], l2_ref[...]
    mx = jnp.maximum(jnp.maximum(l0, l1), l2)
    e0, e1, e2 = jnp.exp(l0 - mx), jnp.exp(l1 - mx), jnp.exp(l2 - mx)
    inv = 1.0 / (e0 + e1 + e2)
    gw = o0_ref.shape[1]
    out_ref[:, 0:gw] = (e0 * inv * o0_ref[...]).astype(out_ref.dtype)
    out_ref[:, gw:2 * gw] = (e1 * inv * o1_ref[...]).astype(out_ref.dtype)
    out_ref[:, 2 * gw:3 * gw] = (e2 * inv * o2_ref[...]).astype(out_ref.dtype)


def merge_groups(outs, lses, tm):
    tokens, gw = outs[0].shape
    spec = pl.BlockSpec((tm, gw), lambda i: (i, 0))
    return pl.pallas_call(
        _merge_kernel,
        grid=(tokens // tm,),
        in_specs=[spec] * 6,
        out_specs=pl.BlockSpec((tm, 3 * gw), lambda i: (i, 0)),
        out_shape=jax.ShapeDtypeStruct((tokens, 3 * gw), BF16),
        compiler_params=_params(1),
        name="merge_dilation_groups",
    )(*outs, *lses)


def _branch_kernel(oa_ref, ob_ref, oc_ref, ga_ref, gb_ref, gc_ref, w_ref, y_ref):
    tm = oa_ref.shape[0]
    ka, kb = oa_ref.shape[1], ob_ref.shape[1]
    for c in range(tm // ROW_CHUNK):
        rows = slice(c * ROW_CHUNK, (c + 1) * ROW_CHUNK)
        ya = jnp.dot(oa_ref[rows, :], w_ref[0:ka, :], preferred_element_type=F32)
        yb = jnp.dot(ob_ref[rows, :], w_ref[ka:ka + kb, :], preferred_element_type=F32)
        yc = jnp.dot(oc_ref[rows, :], w_ref[ka + kb:, :], preferred_element_type=F32)
        y = (ga_ref[rows, :].astype(F32) * ya + gb_ref[rows, :].astype(F32) * yb
             + gc_ref[rows, :].astype(F32) * yc)
        y_ref[rows, :] = y.astype(y_ref.dtype)


def branch_projection(o_a, o_b, o_c, gates, wb, *, tm, tn):
    m = o_a.shape[0]
    kmix, d = wb.shape
    nb = d // tn
    return pl.pallas_call(
        _branch_kernel,
        grid=(m // tm, nb),
        in_specs=[
            pl.BlockSpec((tm, o_a.shape[1]), lambda i, n: (i, 0)),
            pl.BlockSpec((tm, o_b.shape[1]), lambda i, n: (i, 0)),
            pl.BlockSpec((tm, o_c.shape[1]), lambda i, n: (i, 0)),
            pl.BlockSpec((tm, tn), lambda i, n: (i, n)),
            pl.BlockSpec((tm, tn), lambda i, n: (i, nb + n)),
            pl.BlockSpec((tm, tn), lambda i, n: (i, 2 * nb + n)),
            pl.BlockSpec((kmix, tn), lambda i, n: (0, n)),
        ],
        out_specs=pl.BlockSpec((tm, tn), lambda i, n: (i, n)),
        out_shape=jax.ShapeDtypeStruct((m, d), BF16),
        compiler_params=_params(2),
        name="branch_projection",
    )(o_a, o_b, o_c, gates, gates, gates, wb)


def _resid_mm_bf16_kernel(a_ref, w_ref, *refs, alpha):
    *resid, o_ref = refs
    tm = a_ref.shape[0]
    for c in range(tm // ROW_CHUNK):
        rows = slice(c * ROW_CHUNK, (c + 1) * ROW_CHUNK)
        acc = jnp.dot(a_ref[rows, :], w_ref[...], preferred_element_type=F32)
        o_ref[rows, :] = alpha * _residual_rows(resid, rows) + acc


def residual_matmul_bf16(a, wb, resid, *, alpha, tm, tn, name):
    m, k = a.shape
    d = wb.shape[1]
    return pl.pallas_call(
        functools.partial(_resid_mm_bf16_kernel, alpha=alpha),
        grid=(m // tm, d // tn),
        in_specs=[
            pl.BlockSpec((tm, k), lambda i, n: (i, 0)),
            pl.BlockSpec((k, tn), lambda i, n: (0, n)),
        ] + _residual_specs(resid, tm, tn, lambda i, n: i, lambda i, n: n),
        out_specs=pl.BlockSpec((tm, tn), lambda i, n: (i, n)),
        out_shape=jax.ShapeDtypeStruct((m, d), F32),
        compiler_params=_params(2),
        name=name,
    )(a, wb, *resid)


def _layer_norm_rows(x, mu, rstd, g, b):
    return (x - mu) * rstd * g + b


def _residual_rows(resid, rows):
    if len(resid) == 1:
        return resid[0][rows, :]
    pre_ref, mu_ref, rs_ref, g_ref, b_ref = resid
    n = pre_ref.shape[1] // mu_ref.shape[1]
    return _layer_norm_rows(pre_ref[rows, :], _lane_tile(mu_ref[rows, :], n), _lane_tile(rs_ref[rows, :], n),
                            g_ref[...], b_ref[...])


def _residual_specs(resid, tm, tn, row_of, col_of):
    tile = pl.BlockSpec((tm, tn), lambda *a: (row_of(*a), col_of(*a)))
    if len(resid) == 1:
        return [tile]
    stat = pl.BlockSpec((tm, HEAD_DIM), lambda *a: (row_of(*a), 0))
    vec = pl.BlockSpec((1, tn), lambda *a: (0, col_of(*a)))
    return [tile, stat, stat, vec, vec]


def _ln_stats_kernel(x_ref, g_ref, b_ref, ob_ref, mu_ref, rs_ref):
    x = x_ref[...]
    mu = jnp.mean(x, axis=-1, keepdims=True)
    rstd = lax.rsqrt(jnp.mean(jnp.square(x - mu), axis=-1, keepdims=True) + LN_EPS)
    ob_ref[...] = _layer_norm_rows(x, mu, rstd, g_ref[...], b_ref[...]).astype(ob_ref.dtype)
    mu_ref[...] = jnp.broadcast_to(mu, mu_ref.shape)
    rs_ref[...] = jnp.broadcast_to(rstd, rs_ref.shape)


def _ln_kernel(x_ref, g_ref, b_ref, o_ref):
    x = x_ref[...]
    mu = jnp.mean(x, axis=-1, keepdims=True)
    rstd = lax.rsqrt(jnp.mean(jnp.square(x - mu), axis=-1, keepdims=True) + LN_EPS)
    o_ref[...] = _layer_norm_rows(x, mu, rstd, g_ref[...], b_ref[...])


def layer_norm(x, g, b, *, tm, last):
    m, d = x.shape
    spec = pl.BlockSpec((tm, d), lambda i: (i, 0))
    vec = pl.BlockSpec((1, d), lambda i: (0, 0))
    stat = pl.BlockSpec((tm, HEAD_DIM), lambda i: (i, 0))
    g, b = g.reshape(1, d), b.reshape(1, d)
    if last:
        return pl.pallas_call(
            _ln_kernel, grid=(m // tm,), in_specs=[spec, vec, vec], out_specs=spec,
            out_shape=jax.ShapeDtypeStruct((m, d), F32), compiler_params=_params(1), name="layer_norm_out",
        )(x, g, b)
    xb, mu, rstd = pl.pallas_call(
        _ln_stats_kernel,
        grid=(m // tm,),
        in_specs=[spec, vec, vec],
        out_specs=[spec, stat, stat],
        out_shape=[jax.ShapeDtypeStruct((m, d), BF16), jax.ShapeDtypeStruct((m, HEAD_DIM), F32),
                   jax.ShapeDtypeStruct((m, HEAD_DIM), F32)],
        compiler_params=_params(1),
        name="layer_norm",
    )(x, g, b)
    return xb, (x, mu, rstd, g, b)


def _up_kernel(x_ref, w_ref, cwg_ref, cwv_ref, cbg_ref, cbv_ref, wd_ref, o_ref, wdb_ref, halo_ref, *, seq):
    i, n = pl.program_id(0), pl.program_id(1)
    tm, tn = o_ref.shape
    wdb_ref[...] = wd_ref[...].astype(wdb_ref.dtype)

    @pl.when(lax.rem(i * tm, seq) == 0)
    def _():
        halo_ref[n] = jnp.zeros(halo_ref.shape[1:], F32)

    row = lax.broadcasted_iota(jnp.int32, (ROW_CHUNK, tn), 0)
    halo = halo_ref[n]
    for c in range(tm // ROW_CHUNK):
        rows = slice(c * ROW_CHUNK, (c + 1) * ROW_CHUNK)
        h = jnp.dot(x_ref[rows, :], w_ref[...], preferred_element_type=F32)
        prev, halo = halo, h[ROW_CHUNK - 8:, :]
        acts = []
        for half, (cw_ref, cb_ref) in enumerate(((cwg_ref, cbg_ref), (cwv_ref, cbv_ref))):
            hh = h[:, half * tn:(half + 1) * tn]
            hl = prev[:, half * tn:(half + 1) * tn]
            h1 = jnp.where(row == 0, hl[7:8], pltpu.roll(hh, 1, 0))
            h2 = jnp.where(row == 0, hl[6:7], jnp.where(row == 1, hl[7:8], pltpu.roll(hh, 2, 0)))
            cw = cw_ref[...]
            acts.append(cb_ref[...] + cw[0:1] * h2 + cw[1:2] * h1 + cw[2:3] * hh)
        gate, val = acts
        o_ref[rows, :] = (gate * jax.nn.sigmoid(gate) * val).astype(o_ref.dtype)
    halo_ref[n] = halo


def up_conv_glu(xb, wub, conv_w, conv_b, w_down, layer, *, seq, tm, tn):
    m, k = xb.shape
    dff = wub.shape[1] // 2
    nb = dff // tn
    nm = m // tm
    assert seq % tm == 0 and dff % tn == 0
    kd, dd = w_down.shape[-2:]
    slab = kd // (nb * nm)
    assert slab * nb * nm == kd and slab % 16 == 0
    conv_b = conv_b.reshape(conv_b.shape[0], 1, 2 * dff)
    return pl.pallas_call(
        functools.partial(_up_kernel, seq=seq),
        grid=(nm, nb),
        in_specs=[
            pl.BlockSpec((tm, k), lambda i, n: (i, 0)),
            pl.BlockSpec((k, 2 * tn), lambda i, n: (0, n)),
            pl.BlockSpec((None, CONV_WIDTH, tn), lambda i, n: (layer, 0, n)),
            pl.BlockSpec((None, CONV_WIDTH, tn), lambda i, n: (layer, 0, nb + n)),
            pl.BlockSpec((None, 1, tn), lambda i, n: (layer, 0, n)),
            pl.BlockSpec((None, 1, tn), lambda i, n: (layer, 0, nb + n)),
            pl.BlockSpec((None, slab, dd), lambda i, n: (layer, i * nb + n, 0)),
        ],
        out_specs=[pl.BlockSpec((tm, tn), lambda i, n: (i, n)),
                   pl.BlockSpec((slab, dd), lambda i, n: (i * nb + n, 0))],
        out_shape=[jax.ShapeDtypeStruct((m, dff), BF16), jax.ShapeDtypeStruct((kd, dd), BF16)],
        scratch_shapes=[pltpu.VMEM((nb, 8, 2 * tn), F32)],
        compiler_params=_params(2),
        name="up_conv_glu",
    )(xb, wub, conv_w, conv_w, conv_b, conv_b, w_down)


class _Tiles:
    mm_rows = 1024
    qkv_cols = 768
    qkv_c_cols = 768
    gate_cols = 768
    branch_cols = 1024
    out_cols = 1024
    up_rows = 2048
    up_cols = 256
    down_rows = 512
    down_cols = 512
    attn_a_q = 1024
    attn_a_k = 512
    banded_tokens = 512
    strided_tokens = 4096
    ln_rows = 256
    cast_rows = 256


def _forward(x, w_in, diff_lambda, diff_norm_g, sink_logits, w_branch, w_o, ln1_g, ln1_b,
             w_up, conv_w, conv_b, w_down, ln2_g, ln2_b, tiles=_Tiles):
    batch, seq, d = x.shape
    depth = w_in.shape[0]
    tokens = batch * seq
    alpha = (2 * depth) ** 0.25

    slopes = jnp.exp2(-8.0 * (jnp.arange(N_ALIBI_HEADS, dtype=F32) + 1.0) / N_ALIBI_HEADS)
    slopes2 = slopes * LOG2E
    colscale = np.ones((1, QKV_WIDTH), np.float32)
    for off, wdt in ((OFF_AQ, A_WIDTH), (OFF_BQ, B_WIDTH), (OFF_CQ, C_WIDTH)):
        colscale[:, off:off + wdt] = SCALE * LOG2E
    colscale = jnp.asarray(colscale)

    x32 = x.reshape(tokens, d)
    resid = (x32,)
    xb = cast_bf16(x32, tiles.cast_rows)
    for l in range(depth):
        lam_init = 0.8 - 0.6 * math.exp(-0.3 * l)
        qkv, wbb, wob = in_projection(xb, w_in, l, 0, OFF_CQ, colscale=colscale[:, :OFF_CQ], out_dtype=BF16,
                                      tm=tiles.mm_rows, tn=tiles.qkv_cols, name="inproj_qkv",
                                      side=((w_branch, None), (w_o, None)))
        qkv_c = in_projection(xb, w_in, l, OFF_CQ, 3 * C_WIDTH, colscale=colscale[:, OFF_CQ:], out_dtype=F32,
                              tm=tiles.mm_rows, tn=tiles.qkv_c_cols, name="inproj_qkv_c")
        gates, wub = in_projection(xb, w_in, l, QKV_WIDTH, 3 * d, colscale=None, out_dtype=BF16,
                                   tm=tiles.mm_rows, tn=tiles.gate_cols, name="inproj_gates",
                                   side=((w_up, tiles.up_cols),))
        o_a = diff_attention(qkv, slopes2, diff_lambda[l], diff_norm_g[l], batch=batch, seq=seq,
                             lam_init=lam_init, tq=tiles.attn_a_q, tk=tiles.attn_a_k)
        (o_b,) = banded_attention(
            qkv, slopes2, sink_logits[l] * LOG2E, batch=batch, seq=seq, dil=1, tq=tiles.banded_tokens,
            hk=1, g=B_Q_HEADS // B_KV_HEADS, n_steps=B_KV_HEADS, q_off=OFF_BQ, k_off=OFF_BK, v_off=OFF_BV,
            max_dist=B_WINDOW - 1, slope0=0, out_width=B_WIDTH, out_dtype=BF16, want_lse=False)
        c_outs, c_lses = [], []
        gw = C_HEADS_PER_GROUP * HEAD_DIM
        for gi, (window, dil) in enumerate(C_GROUPS):
            hk = C_HEADS_PER_GROUP if dil == 1 else 1
            o, lse = banded_attention(
                qkv_c, slopes2, None, batch=batch, seq=seq, dil=dil,
                tq=tiles.banded_tokens if dil == 1 else max(tiles.strided_tokens // dil, BLOCK),
                hk=hk, g=1, n_steps=C_HEADS_PER_GROUP // hk,
                q_off=gi * gw, k_off=C_WIDTH + gi * gw, v_off=2 * C_WIDTH + gi * gw,
                max_dist=window // dil, slope0=B_Q_HEADS + gi * C_HEADS_PER_GROUP,
                out_width=gw, out_dtype=F32, want_lse=True)
            c_outs.append(o)
            c_lses.append(lse)
        o_c = merge_groups(c_outs, c_lses, tiles.ln_rows)
        y = branch_projection(o_a, o_b, o_c, gates, wbb, tm=tiles.mm_rows, tn=tiles.branch_cols)
        pre = residual_matmul_bf16(y, wob, resid, alpha=alpha, tm=tiles.mm_rows, tn=tiles.out_cols,
                                   name="out_projection")
        xb, resid = layer_norm(pre, ln1_g[l], ln1_b[l], tm=tiles.ln_rows, last=False)
        act, wdb = up_conv_glu(xb, wub, conv_w, conv_b, w_down, l, seq=seq, tm=tiles.up_rows, tn=tiles.up_cols)
        pre = residual_matmul_bf16(act, wdb, resid, alpha=alpha, tm=tiles.down_rows, tn=tiles.down_cols,
                                   name="down_projection")
        if l + 1 < depth:
            xb, resid = layer_norm(pre, ln2_g[l], ln2_b[l], tm=tiles.ln_rows, last=False)
    out = layer_norm(pre, ln2_g[depth - 1], ln2_b[depth - 1], tm=tiles.ln_rows, last=True)
    return out.reshape(batch, seq, d)


def kernel(x, w_in, diff_lambda, diff_norm_g, sink_logits, w_branch, w_o, ln1_g, ln1_b,
           w_up, conv_w, conv_b, w_down, ln2_g, ln2_b):
    return _forward(x, w_in, diff_lambda, diff_norm_g, sink_logits, w_branch, w_o, ln1_g, ln1_b,
                    w_up, conv_w, conv_b, w_down, ln2_g, ln2_b)
```

```python
import functools
import math

import numpy as np
import jax
import jax.numpy as jnp
from jax import lax
from jax.experimental import pallas as pl
from jax.experimental.pallas import tpu as pltpu

F32 = jnp.float32
BF16 = jnp.bfloat16

HEAD_DIM = 128
BLOCK = 128
SCALE = HEAD_DIM ** -0.5
A_HEADS = 6
A_V_DIM = 2 * HEAD_DIM
B_Q_HEADS = 8
B_KV_HEADS = 2
B_WINDOW = 128
C_GROUPS = ((128, 1), (512, 4), (2048, 16))
C_HEADS_PER_GROUP = 4
C_HEADS = C_HEADS_PER_GROUP * len(C_GROUPS)
N_ALIBI_HEADS = B_Q_HEADS + C_HEADS + A_HEADS
A_WIDTH = A_HEADS * A_V_DIM
B_WIDTH = B_Q_HEADS * HEAD_DIM
C_WIDTH = C_HEADS * HEAD_DIM
CONV_WIDTH = 3
LN_EPS = 1e-5
RMS_EPS = 1e-5
LOG2E = math.log2(math.e)
LN2 = math.log(2.0)
NEG_BIG = -1e30

OFF_AQ = 0
OFF_AK = OFF_AQ + A_WIDTH
OFF_AV = OFF_AK + A_WIDTH
OFF_BQ = OFF_AV + A_WIDTH
OFF_BK = OFF_BQ + B_WIDTH
OFF_BV = OFF_BK + B_KV_HEADS * HEAD_DIM
OFF_CQ = OFF_BV + B_KV_HEADS * HEAD_DIM
OFF_CK = OFF_CQ + C_WIDTH
OFF_CV = OFF_CK + C_WIDTH
QKV_WIDTH = OFF_CV + C_WIDTH

V7X_VMEM_LIMIT_BYTES = 60 * 1024 * 1024
ROW_CHUNK = 256


def _params(n_axes, vmem=V7X_VMEM_LIMIT_BYTES):
    return pltpu.CompilerParams(dimension_semantics=("arbitrary",) * n_axes, vmem_limit_bytes=vmem)


def _cast_kernel(x_ref, o_ref):
    o_ref[...] = x_ref[...].astype(o_ref.dtype)


def cast_bf16(x, rows):
    r, c = x.shape
    return pl.pallas_call(
        _cast_kernel,
        grid=(r // rows,),
        in_specs=[pl.BlockSpec((rows, c), lambda i: (i, 0))],
        out_specs=pl.BlockSpec((rows, c), lambda i: (i, 0)),
        out_shape=jax.ShapeDtypeStruct((r, c), BF16),
        compiler_params=_params(1),
        name="cast_bf16",
    )(x)


def _cast_weight(w_ref, wb_ref):
    def body(i, c):
        r = pl.multiple_of(i * ROW_CHUNK, ROW_CHUNK)
        wb_ref[pl.ds(r, ROW_CHUNK), :] = w_ref[pl.ds(r, ROW_CHUNK), :].astype(BF16)
        return c

    lax.fori_loop(0, w_ref.shape[0] // ROW_CHUNK, body, 0)


def _inproj_kernel(x_ref, w_ref, *refs, has_scale, pair_widths):
    n_side = len(pair_widths)
    cs_ref = refs[0] if has_scale else None
    refs = refs[1:] if has_scale else refs
    side_in, o_ref, side_out, wb_ref = refs[:n_side], refs[n_side], refs[n_side + 1:-1], refs[-1]
    for s_in, s_out, pw in zip(side_in, side_out, pair_widths):
        if pw is None:
            s_out[...] = s_in[...].astype(s_out.dtype)
        else:
            half = s_in.shape[1] // 2
            for j in range(half // pw):
                s_out[:, 2 * j * pw:(2 * j + 1) * pw] = s_in[:, j * pw:(j + 1) * pw].astype(s_out.dtype)
                s_out[:, (2 * j + 1) * pw:(2 * j + 2) * pw] = (
                    s_in[:, half + j * pw:half + (j + 1) * pw].astype(s_out.dtype))

    @pl.when(pl.program_id(1) == 0)
    def _():
        _cast_weight(w_ref, wb_ref)

    for c in range(x_ref.shape[0] // ROW_CHUNK):
        rows = slice(c * ROW_CHUNK, (c + 1) * ROW_CHUNK)
        acc = jnp.dot(x_ref[rows, :], wb_ref[...], preferred_element_type=F32)
        acc = acc * cs_ref[...] if has_scale else jax.nn.sigmoid(acc)
        o_ref[rows, :] = acc.astype(o_ref.dtype)


def in_projection(xb, w, layer, col0, width, *, colscale, out_dtype, tm, tn, side=(), name):
    m, k = xb.shape
    assert col0 % tn == 0 and width % tn == 0 and m % tm == 0
    cb = col0 // tn
    nm = m // tm
    steps = (width // tn) * nm
    in_specs = [
        pl.BlockSpec((tm, k), lambda n, i: (i, 0)),
        pl.BlockSpec((None, k, tn), lambda n, i: (layer, 0, cb + n)),
    ]
    args = [xb, w]
    out_specs = [pl.BlockSpec((tm, tn), lambda n, i: (i, n))]
    out_shapes = [jax.ShapeDtypeStruct((m, width), out_dtype)]
    if colscale is not None:
        in_specs.append(pl.BlockSpec((1, tn), lambda n, i: (0, n)))
        args.append(colscale)
    for arr, _ in side:
        r, c = arr.shape[-2:]
        slab = r // steps
        assert slab * steps == r and slab % 16 == 0
        in_specs.append(pl.BlockSpec((None, slab, c), lambda n, i: (layer, n * nm + i, 0)))
        out_specs.append(pl.BlockSpec((slab, c), lambda n, i: (n * nm + i, 0)))
        out_shapes.append(jax.ShapeDtypeStruct((r, c), BF16))
        args.append(arr)
    res = pl.pallas_call(
        functools.partial(_inproj_kernel, has_scale=colscale is not None,
                          pair_widths=tuple(pw for _, pw in side)),
        grid=(width // tn, nm),
        in_specs=in_specs,
        out_specs=out_specs,
        out_shape=out_shapes,
        scratch_shapes=[pltpu.VMEM((k, tn), BF16)],
        compiler_params=_params(2),
        name=name,
    )(*args)
    return res if side else res[0]


def _lane_tile(x, n):
    return jnp.concatenate([x] * n, axis=1)


def _attn_a_kernel(slopes_ref, q_ref, k_ref, v_ref, lp_ref, g_ref, o_ref, m_ref, l_ref, acc_ref,
                   *, tq, tk, lam_init, slope0):
    h = pl.program_id(1)
    qi = pl.program_id(2)
    slope = slopes_ref[slope0 + h]
    col = lax.broadcasted_iota(jnp.int32, (1, tk), 1)
    nt = (((1,), (1,)), ((), ()))

    m_ref[...] = jnp.full(m_ref.shape, NEG_BIG, F32)
    l_ref[...] = jnp.zeros(l_ref.shape, F32)
    acc_ref[...] = jnp.zeros(acc_ref.shape, F32)

    def step(r0, js, rel, masked):
        rows = slice(r0, tq)
        vj = v_ref[pl.ds(js, tk), :]
        cb = slope * (col + rel).astype(F32)
        for mi in range(2):
            hd = slice(mi * HEAD_DIM, (mi + 1) * HEAD_DIM)
            s = lax.dot_general(q_ref[rows, hd], k_ref[pl.ds(js, tk), hd], nt, preferred_element_type=F32) + cb
            if masked:
                keep = (lax.broadcasted_iota(jnp.int32, (tq - r0, tk), 0)
                        >= lax.broadcasted_iota(jnp.int32, (tq - r0, tk), 1))
                s = jnp.where(keep, s, NEG_BIG)
            mo = m_ref[mi, rows]
            mn = jnp.maximum(mo, jnp.max(s, axis=-1, keepdims=True))
            a = jnp.exp2(mo - mn)
            p = jnp.exp2(s - _lane_tile(mn, tk // HEAD_DIM))
            l_ref[mi, rows] = a * l_ref[mi, rows] + jnp.sum(p, axis=-1, keepdims=True)
            m_ref[mi, rows] = mn
            acc_ref[mi, rows] = (_lane_tile(a, A_V_DIM // HEAD_DIM) * acc_ref[mi, rows]
                                 + jnp.dot(p.astype(BF16), vj, preferred_element_type=F32))

    def body(j, c):
        js = pl.multiple_of(j * tk, tk)
        step(0, js, js - qi * tq, False)
        return c

    lax.fori_loop(0, qi * (tq // tk), body, 0)
    for jj in range(tq // tk):
        step(jj * tk, pl.multiple_of(qi * tq + jj * tk, tk), jj * tk, True)

    lp = lp_ref[...]
    lam = (jnp.exp(jnp.sum(lp[0:1] * lp[1:2], axis=-1, keepdims=True))
           - jnp.exp(jnp.sum(lp[2:3] * lp[3:4], axis=-1, keepdims=True)) + lam_init)
    rep = A_V_DIM // HEAD_DIM
    o = (acc_ref[0] * _lane_tile(1.0 / l_ref[0], rep)
         - lam * (acc_ref[1] * _lane_tile(1.0 / l_ref[1], rep)))
    o = o * lax.rsqrt(jnp.mean(jnp.square(o), axis=-1, keepdims=True) + RMS_EPS) * g_ref[...]
    o_ref[...] = (o * (1.0 - lam_init)).astype(o_ref.dtype)


def diff_attention(qkv, slopes2, lam_params, norm_g, *, batch, seq, lam_init, tq, tk):
    nq = seq // tq
    kb = OFF_AK // A_V_DIM
    vb = OFF_AV // A_V_DIM
    assert tq % tk == 0 and seq % tq == 0
    return pl.pallas_call(
        functools.partial(_attn_a_kernel, tq=tq, tk=tk, lam_init=lam_init, slope0=B_Q_HEADS + C_HEADS),
        grid=(batch, A_HEADS, nq),
        in_specs=[
            pl.BlockSpec(memory_space=pltpu.SMEM),
            pl.BlockSpec((tq, A_V_DIM), lambda b, h, i: (b * nq + i, h)),
            pl.BlockSpec((seq, A_V_DIM), lambda b, h, i: (b, kb + h)),
            pl.BlockSpec((seq, A_V_DIM), lambda b, h, i: (b, vb + h)),
            pl.BlockSpec((4, HEAD_DIM), lambda b, h, i: (0, 0)),
            pl.BlockSpec((1, A_V_DIM), lambda b, h, i: (0, 0)),
        ],
        out_specs=pl.BlockSpec((tq, A_V_DIM), lambda b, h, i: (b * nq + i, h)),
        out_shape=jax.ShapeDtypeStruct((batch * seq, A_WIDTH), BF16),
        scratch_shapes=[pltpu.VMEM((2, tq, HEAD_DIM), F32), pltpu.VMEM((2, tq, HEAD_DIM), F32),
                        pltpu.VMEM((2, tq, A_V_DIM), F32)],
        compiler_params=_params(3),
        name="diff_attention",
    )(slopes2, qkv, qkv, qkv, lam_params, norm_g.reshape(1, A_V_DIM))


def _banded_kernel(*refs, dil, tq, hk, g, max_dist, slope0, has_sink, want_lse, whole_seq):
    it = iter(refs)
    slopes_ref = next(it)
    sinks_ref = next(it) if has_sink else None
    q_ref, kp_ref, k_ref, vp_ref, v_ref = (next(it) for _ in range(5))
    o_ref = next(it)
    lse_ref = next(it) if want_lse else None

    khs = pl.program_id(1)
    qi = pl.program_id(2)
    nt = (((1,), (1,)), ((), ()))

    def rows_of(blk, r):
        if dil == 1:
            return slice(blk * BLOCK, (blk + 1) * BLOCK)
        return pl.ds(blk * BLOCK * dil + r, BLOCK, stride=dil)

    def residue(r):
        row = lax.broadcasted_iota(jnp.int32, (BLOCK, 2 * BLOCK), 0)
        col = lax.broadcasted_iota(jnp.int32, (BLOCK, 2 * BLOCK), 1)
        dist = row + BLOCK - col
        band = (dist >= 0) & (dist <= max_dist)
        first_key = jnp.where(qi > 0, 0, BLOCK)
        band_first = band & (col >= first_key)
        dist_f = (dist * dil).astype(F32)
        for sb in range(tq // BLOCK):
            rows = rows_of(sb, r)
            valid = band_first if sb == 0 else band
            for kh in range(hk):
                kc = slice(kh * HEAD_DIM, (kh + 1) * HEAD_DIM)
                if sb == 0 and whole_seq:
                    kprev = vprev = jnp.zeros((BLOCK, HEAD_DIM), k_ref.dtype)
                elif sb == 0:
                    kprev, vprev = kp_ref[rows_of(0, r), kc], vp_ref[rows_of(0, r), kc]
                else:
                    kprev, vprev = k_ref[rows_of(sb - 1, r), kc], v_ref[rows_of(sb - 1, r), kc]
                kk = jnp.concatenate([kprev, k_ref[rows, kc]], axis=0).astype(BF16)
                vv = jnp.concatenate([vprev, v_ref[rows, kc]], axis=0).astype(BF16)
                hc = [slice((kh * g + gi) * HEAD_DIM, (kh * g + gi + 1) * HEAD_DIM) for gi in range(g)]
                qs = jnp.concatenate([q_ref[rows, c] for c in hc], axis=0).astype(BF16)
                s_all = lax.dot_general(qs, kk, nt, preferred_element_type=F32)
                es, invs = [], []
                for gi in range(g):
                    head = (khs * hk + kh) * g + gi
                    slope = slopes_ref[slope0 + head]
                    s = s_all[gi * BLOCK:(gi + 1) * BLOCK] - slope * dist_f
                    s = jnp.where(valid, s, NEG_BIG)
                    m = jnp.max(s, axis=-1, keepdims=True)
                    if has_sink:
                        sk = sinks_ref[head]
                        m = jnp.maximum(m, sk)
                    e = jnp.exp2(s - m)
                    den = jnp.sum(e, axis=-1, keepdims=True)
                    if has_sink:
                        den = den + jnp.exp2(sk - m)
                    es.append(e.astype(BF16))
                    invs.append(1.0 / den)
                    if want_lse:
                        lse = (m + jnp.log2(den)) * LN2
                        lse_ref[rows, hc[gi]] = jnp.broadcast_to(lse, (BLOCK, HEAD_DIM))
                o_all = jnp.dot(jnp.concatenate(es, axis=0), vv, preferred_element_type=F32)
                for gi in range(g):
                    o = o_all[gi * BLOCK:(gi + 1) * BLOCK] * invs[gi]
                    o_ref[rows, hc[gi]] = o.astype(o_ref.dtype)

    if dil == 1:
        residue(0)
    else:
        def body(r, c):
            residue(r)
            return c

        lax.fori_loop(0, dil, body, 0)


def banded_attention(qkv, slopes2, sinks2, *, batch, seq, dil, tq, hk, g, n_steps, q_off, k_off, v_off,
                     max_dist, slope0, out_width, out_dtype, want_lse):
    tokens = qkv.shape[0]
    tt = tq * dil
    pt = BLOCK * dil
    nq = seq // tt
    qw = hk * g * HEAD_DIM
    kw = hk * HEAD_DIM
    qcb, kcb, vcb = q_off // qw, k_off // kw, v_off // kw
    assert q_off % qw == 0 and k_off % kw == 0 and v_off % kw == 0 and seq % tt == 0 and tt % pt == 0

    def prev_row(b, i):
        return jnp.maximum(b * (seq // pt) + i * (tt // pt) - 1, 0)

    has_sink = sinks2 is not None
    in_specs = [pl.BlockSpec(memory_space=pltpu.SMEM)]
    args = [slopes2]
    if has_sink:
        in_specs.append(pl.BlockSpec(memory_space=pltpu.SMEM))
        args.append(sinks2)
    in_specs += [
        pl.BlockSpec((tt, qw), lambda b, s, i: (b * nq + i, qcb + s)),
        pl.BlockSpec((pt, kw), lambda b, s, i: (prev_row(b, i), kcb + s)),
        pl.BlockSpec((tt, kw), lambda b, s, i: (b * nq + i, kcb + s)),
        pl.BlockSpec((pt, kw), lambda b, s, i: (prev_row(b, i), vcb + s)),
        pl.BlockSpec((tt, kw), lambda b, s, i: (b * nq + i, vcb + s)),
    ]
    args += [qkv] * 5
    out_spec = pl.BlockSpec((tt, qw), lambda b, s, i: (b * nq + i, s))
    out_shapes = [jax.ShapeDtypeStruct((tokens, out_width), out_dtype)]
    out_specs = [out_spec]
    if want_lse:
        out_shapes.append(jax.ShapeDtypeStruct((tokens, out_width), F32))
        out_specs.append(out_spec)
    return pl.pallas_call(
        functools.partial(_banded_kernel, dil=dil, tq=tq, hk=hk, g=g, max_dist=max_dist, slope0=slope0,
                          has_sink=has_sink, want_lse=want_lse, whole_seq=nq == 1),
        grid=(batch, n_steps, nq),
        in_specs=in_specs,
        out_specs=out_specs,
        out_shape=out_shapes,
        compiler_params=_params(3),
        name=f"banded_attention_d{dil}_g{g}",
    )(*args)


def _merge_kernel(o0_ref, o1_ref, o2_ref, l0_ref, l1_ref, l2_ref, out_ref):
    l0, l1, l2 = l0_ref[...], l1_ref[...], l2_ref[...]
    mx = jnp.maximum(jnp.maximum(l0, l1), l2)
    e0, e1, e2 = jnp.exp(l0 - mx), jnp.exp(l1 - mx), jnp.exp(l2 - mx)
    inv = 1.0 / (e0 + e1 + e2)
    gw = o0_ref.shape[1]
    out_ref[:, 0:gw] = (e0 * inv * o0_ref[...]).astype(out_ref.dtype)
    out_ref[:, gw:2 * gw] = (e1 * inv * o1_ref[...]).astype(out_ref.dtype)
    out_ref[:, 2 * gw:3 * gw] = (e2 * inv * o2_ref[...]).astype(out_ref.dtype)


def merge_groups(outs, lses, tm):
    tokens, gw = outs[0].shape
    spec = pl.BlockSpec((tm, gw), lambda i: (i, 0))
    return pl.pallas_call(
        _merge_kernel,
        grid=(tokens // tm,),
        in_specs=[spec] * 6,
        out_specs=pl.BlockSpec((tm, 3 * gw), lambda i: (i, 0)),
        out_shape=jax.ShapeDtypeStruct((tokens, 3 * gw), BF16),
        compiler_params=_params(1),
        name="merge_dilation_groups",
    )(*outs, *lses)


def _branch_kernel(oa_ref, ob_ref, oc_ref, ga_ref, gb_ref, gc_ref, w_ref, y_ref):
    tm = oa_ref.shape[0]
    ka, kb = oa_ref.shape[1], ob_ref.shape[1]
    for c in range(tm // ROW_CHUNK):
        rows = slice(c * ROW_CHUNK, (c + 1) * ROW_CHUNK)
        ya = jnp.dot(oa_ref[rows, :], w_ref[0:ka, :], preferred_element_type=F32)
        yb = jnp.dot(ob_ref[rows, :], w_ref[ka:ka + kb, :], preferred_element_type=F32)
        yc = jnp.dot(oc_ref[rows, :], w_ref[ka + kb:, :], preferred_element_type=F32)
        y = (ga_ref[rows, :].astype(F32) * ya + gb_ref[rows, :].astype(F32) * yb
             + gc_ref[rows, :].astype(F32) * yc)
        y_ref[rows, :] = y.astype(y_ref.dtype)


def branch_projection(o_a, o_b, o_c, gates, wb, *, tm, tn):
    m = o_a.shape[0]
    kmix, d = wb.shape
    nb = d // tn
    return pl.pallas_call(
        _branch_kernel,
        grid=(m // tm, nb),
        in_specs=[
            pl.BlockSpec((tm, o_a.shape[1]), lambda i, n: (i, 0)),
            pl.BlockSpec((tm, o_b.shape[1]), lambda i, n: (i, 0)),
            pl.BlockSpec((tm, o_c.shape[1]), lambda i, n: (i, 0)),
            pl.BlockSpec((tm, tn), lambda i, n: (i, n)),
            pl.BlockSpec((tm, tn), lambda i, n: (i, nb + n)),
            pl.BlockSpec((tm, tn), lambda i, n: (i, 2 * nb + n)),
            pl.BlockSpec((kmix, tn), lambda i, n: (0, n)),
        ],
        out_specs=pl.BlockSpec((tm, tn), lambda i, n: (i, n)),
        out_shape=jax.ShapeDtypeStruct((m, d), BF16),
        compiler_params=_params(2),
        name="branch_projection",
    )(o_a, o_b, o_c, gates, gates, gates, wb)


def _resid_mm_bf16_kernel(a_ref, w_ref, *refs, alpha):
    *resid, o_ref = refs
    tm = a_ref.shape[0]
    for c in range(tm // ROW_CHUNK):
        rows = slice(c * ROW_CHUNK, (c + 1) * ROW_CHUNK)
        acc = jnp.dot(a_ref[rows, :], w_ref[...], preferred_element_type=F32)
        o_ref[rows, :] = alpha * _residual_rows(resid, rows) + acc


def residual_matmul_bf16(a, wb, resid, *, alpha, tm, tn, name):
    m, k = a.shape
    d = wb.shape[1]
    return pl.pallas_call(
        functools.partial(_resid_mm_bf16_kernel, alpha=alpha),
        grid=(m // tm, d // tn),
        in_specs=[
            pl.BlockSpec((tm, k), lambda i, n: (i, 0)),
            pl.BlockSpec((k, tn), lambda i, n: (0, n)),
        ] + _residual_specs(resid, tm, tn, lambda i, n: i, lambda i, n: n),
        out_specs=pl.BlockSpec((tm, tn), lambda i, n: (i, n)),
        out_shape=jax.ShapeDtypeStruct((m, d), F32),
        compiler_params=_params(2),
        name=name,
    )(a, wb, *resid)


def _layer_norm_rows(x, mu, rstd, g, b):
    return (x - mu) * rstd * g + b


def _residual_rows(resid, rows):
    if len(resid) == 1:
        return resid[0][rows, :]
    pre_ref, mu_ref, rs_ref, g_ref, b_ref = resid
    n = pre_ref.shape[1] // mu_ref.shape[1]
    return _layer_norm_rows(pre_ref[rows, :], _lane_tile(mu_ref[rows, :], n), _lane_tile(rs_ref[rows, :], n),
                            g_ref[...], b_ref[...])


def _residual_specs(resid, tm, tn, row_of, col_of):
    tile = pl.BlockSpec((tm, tn), lambda *a: (row_of(*a), col_of(*a)))
    if len(resid) == 1:
        return [tile]
    stat = pl.BlockSpec((tm, HEAD_DIM), lambda *a: (row_of(*a), 0))
    vec = pl.BlockSpec((1, tn), lambda *a: (0, col_of(*a)))
    return [tile, stat, stat, vec, vec]


def _ln_stats_kernel(x_ref, g_ref, b_ref, ob_ref, mu_ref, rs_ref):
    x = x_ref[...]
    mu = jnp.mean(x, axis=-1, keepdims=True)
    rstd = lax.rsqrt(jnp.mean(jnp.square(x - mu), axis=-1, keepdims=True) + LN_EPS)
    ob_ref[...] = _layer_norm_rows(x, mu, rstd, g_ref[...], b_ref[...]).astype(ob_ref.dtype)
    mu_ref[...] = jnp.broadcast_to(mu, mu_ref.shape)
    rs_ref[...] = jnp.broadcast_to(rstd, rs_ref.shape)


def _ln_kernel(x_ref, g_ref, b_ref, o_ref):
    x = x_ref[...]
    mu = jnp.mean(x, axis=-1, keepdims=True)
    rstd = lax.rsqrt(jnp.mean(jnp.square(x - mu), axis=-1, keepdims=True) + LN_EPS)
    o_ref[...] = _layer_norm_rows(x, mu, rstd, g_ref[...], b_ref[...])


def layer_norm(x, g, b, *, tm, last):
    m, d = x.shape
    spec = pl.BlockSpec((tm, d), lambda i: (i, 0))
    vec = pl.BlockSpec((1, d), lambda i: (0, 0))
    stat = pl.BlockSpec((tm, HEAD_DIM), lambda i: (i, 0))
    g, b = g.reshape(1, d), b.reshape(1, d)
    if last:
        return pl.pallas_call(
            _ln_kernel, grid=(m // tm,), in_specs=[spec, vec, vec], out_specs=spec,
            out_shape=jax.ShapeDtypeStruct((m, d), F32), compiler_params=_params(1), name="layer_norm_out",
        )(x, g, b)
    xb, mu, rstd = pl.pallas_call(
        _ln_stats_kernel,
        grid=(m // tm,),
        in_specs=[spec, vec, vec],
        out_specs=[spec, stat, stat],
        out_shape=[jax.ShapeDtypeStruct((m, d), BF16), jax.ShapeDtypeStruct((m, HEAD_DIM), F32),
                   jax.ShapeDtypeStruct((m, HEAD_DIM), F32)],
        compiler_params=_params(1),
        name="layer_norm",
    )(x, g, b)
    return xb, (x, mu, rstd, g, b)


def _up_kernel(x_ref, w_ref, cwg_ref, cwv_ref, cbg_ref, cbv_ref, wd_ref, o_ref, wdb_ref, halo_ref, *, seq):
    i, n = pl.program_id(0), pl.program_id(1)
    tm, tn = o_ref.shape
    wdb_ref[...] = wd_ref[...].astype(wdb_ref.dtype)

    @pl.when(lax.rem(i * tm, seq) == 0)
    def _():
        halo_ref[n] = jnp.zeros(halo_ref.shape[1:], F32)

    row = lax.broadcasted_iota(jnp.int32, (ROW_CHUNK, tn), 0)
    halo = halo_ref[n]
    for c in range(tm // ROW_CHUNK):
        rows = slice(c * ROW_CHUNK, (c + 1) * ROW_CHUNK)
        h = jnp.dot(x_ref[rows, :], w_ref[...], preferred_element_type=F32)
        prev, halo = halo, h[ROW_CHUNK - 8:, :]
        acts = []
        for half, (cw_ref, cb_ref) in enumerate(((cwg_ref, cbg_ref), (cwv_ref, cbv_ref))):
            hh = h[:, half * tn:(half + 1) * tn]
            hl = prev[:, half * tn:(half + 1) * tn]
            h1 = jnp.where(row == 0, hl[7:8], pltpu.roll(hh, 1, 0))
            h2 = jnp.where(row == 0, hl[6:7], jnp.where(row == 1, hl[7:8], pltpu.roll(hh, 2, 0)))
            cw = cw_ref[...]
            acts.append(cb_ref[...] + cw[0:1] * h2 + cw[1:2] * h1 + cw[2:3] * hh)
        gate, val = acts
        o_ref[rows, :] = (gate * jax.nn.sigmoid(gate) * val).astype(o_ref.dtype)
    halo_ref[n] = halo


def up_conv_glu(xb, wub, conv_w, conv_b, w_down, layer, *, seq, tm, tn):
    m, k = xb.shape
    dff = wub.shape[1] // 2
    nb = dff // tn
    nm = m // tm
    assert seq % tm == 0 and dff % tn == 0
    kd, dd = w_down.shape[-2:]
    slab = kd // (nb * nm)
    assert slab * nb * nm == kd and slab % 16 == 0
    conv_b = conv_b.reshape(conv_b.shape[0], 1, 2 * dff)
    return pl.pallas_call(
        functools.partial(_up_kernel, seq=seq),
        grid=(nm, nb),
        in_specs=[
            pl.BlockSpec((tm, k), lambda i, n: (i, 0)),
            pl.BlockSpec((k, 2 * tn), lambda i, n: (0, n)),
            pl.BlockSpec((None, CONV_WIDTH, tn), lambda i, n: (layer, 0, n)),
            pl.BlockSpec((None, CONV_WIDTH, tn), lambda i, n: (layer, 0, nb + n)),
            pl.BlockSpec((None, 1, tn), lambda i, n: (layer, 0, n)),
            pl.BlockSpec((None, 1, tn), lambda i, n: (layer, 0, nb + n)),
            pl.BlockSpec((None, slab, dd), lambda i, n: (layer, i * nb + n, 0)),
        ],
        out_specs=[pl.BlockSpec((tm, tn), lambda i, n: (i, n)),
                   pl.BlockSpec((slab, dd), lambda i, n: (i * nb + n, 0))],
        out_shape=[jax.ShapeDtypeStruct((m, dff), BF16), jax.ShapeDtypeStruct((kd, dd), BF16)],
        scratch_shapes=[pltpu.VMEM((nb, 8, 2 * tn), F32)],
        compiler_params=_params(2),
        name="up_conv_glu",
    )(xb, wub, conv_w, conv_w, conv_b, conv_b, w_down)


class _Tiles:
    mm_rows = 1024
    qkv_cols = 768
    qkv_c_cols = 768
    gate_cols = 768
    branch_cols = 1024
    out_cols = 1024
    up_rows = 2048
    up_cols = 256
    down_rows = 512
    down_cols = 512
    attn_a_q = 2048
    attn_a_k = 512
    banded_tokens = 512
    strided_tokens = 4096
    ln_rows = 512
    cast_rows = 512


def _forward(x, w_in, diff_lambda, diff_norm_g, sink_logits, w_branch, w_o, ln1_g, ln1_b,
             w_up, conv_w, conv_b, w_down, ln2_g, ln2_b, tiles=_Tiles):
    batch, seq, d = x.shape
    depth = w_in.shape[0]
    tokens = batch * seq
    alpha = (2 * depth) ** 0.25

    slopes = jnp.exp2(-8.0 * (jnp.arange(N_ALIBI_HEADS, dtype=F32) + 1.0) / N_ALIBI_HEADS)
    slopes2 = slopes * LOG2E
    colscale = np.ones((1, QKV_WIDTH), np.float32)
    for off, wdt in ((OFF_AQ, A_WIDTH), (OFF_BQ, B_WIDTH), (OFF_CQ, C_WIDTH)):
        colscale[:, off:off + wdt] = SCALE * LOG2E
    colscale = jnp.asarray(colscale)

    x32 = x.reshape(tokens, d)
    resid = (x32,)
    xb = cast_bf16(x32, tiles.cast_rows)
    for l in range(depth):
        lam_init = 0.8 - 0.6 * math.exp(-0.3 * l)
        qkv, wbb, wob = in_projection(xb, w_in, l, 0, OFF_CQ, colscale=colscale[:, :OFF_CQ], out_dtype=BF16,
                                      tm=tiles.mm_rows, tn=tiles.qkv_cols, name="inproj_qkv",
                                      side=((w_branch, None), (w_o, None)))
        qkv_c = in_projection(xb, w_in, l, OFF_CQ, 3 * C_WIDTH, colscale=colscale[:, OFF_CQ:], out_dtype=F32,
                              tm=tiles.mm_rows, tn=tiles.qkv_c_cols, name="inproj_qkv_c")
        gates, wub = in_projection(xb, w_in, l, QKV_WIDTH, 3 * d, colscale=None, out_dtype=BF16,
                                   tm=tiles.mm_rows, tn=tiles.gate_cols, name="inproj_gates",
                                   side=((w_up, tiles.up_cols),))
        o_a = diff_attention(qkv, slopes2, diff_lambda[l], diff_norm_g[l], batch=batch, seq=seq,
                             lam_init=lam_init, tq=tiles.attn_a_q, tk=tiles.attn_a_k)
        (o_b,) = banded_attention(
            qkv, slopes2, sink_logits[l] * LOG2E, batch=batch, seq=seq, dil=1, tq=tiles.banded_tokens,
            hk=1, g=B_Q_HEADS // B_KV_HEADS, n_steps=B_KV_HEADS, q_off=OFF_BQ, k_off=OFF_BK, v_off=OFF_BV,
            max_dist=B_WINDOW - 1, slope0=0, out_width=B_WIDTH, out_dtype=BF16, want_lse=False)
        c_outs, c_lses = [], []
        gw = C_HEADS_PER_GROUP * HEAD_DIM
        for gi, (window, dil) in enumerate(C_GROUPS):
            hk = C_HEADS_PER_GROUP if dil == 1 else 1
            o, lse = banded_attention(
                qkv_c, slopes2, None, batch=batch, seq=seq, dil=dil,
                tq=tiles.banded_tokens if dil == 1 else max(tiles.strided_tokens // dil, BLOCK),
                hk=hk, g=1, n_steps=C_HEADS_PER_GROUP // hk,
                q_off=gi * gw, k_off=C_WIDTH + gi * gw, v_off=2 * C_WIDTH + gi * gw,
                max_dist=window // dil, slope0=B_Q_HEADS + gi * C_HEADS_PER_GROUP,
                out_width=gw, out_dtype=F32, want_lse=True)
            c_outs.append(o)
            c_lses.append(lse)
        o_c = merge_groups(c_outs, c_lses, tiles.ln_rows)
        y = branch_projection(o_a, o_b, o_c, gates, wbb, tm=tiles.mm_rows, tn=tiles.branch_cols)
        pre = residual_matmul_bf16(y, wob, resid, alpha=alpha, tm=tiles.mm_rows, tn=tiles.out_cols,
                                   name="out_projection")
        xb, resid = layer_norm(pre, ln1_g[l], ln1_b[l], tm=tiles.ln_rows, last=False)
        act, wdb = up_conv_glu(xb, wub, conv_w, conv_b, w_down, l, seq=seq, tm=tiles.up_rows, tn=tiles.up_cols)
        pre = residual_matmul_bf16(act, wdb, resid, alpha=alpha, tm=tiles.down_rows, tn=tiles.down_cols,
                                   name="down_projection")
        if l + 1 < depth:
            xb, resid = layer_norm(pre, ln2_g[l], ln2_b[l], tm=tiles.ln_rows, last=False)
    out = layer_norm(pre, ln2_g[depth - 1], ln2_b[depth - 1], tm=tiles.ln_rows, last=True)
    return out.reshape(batch, seq, d)


def kernel(x, w_in, diff_lambda, diff_norm_g, sink_logits, w_branch, w_o, ln1_g, ln1_b,
           w_up, conv_w, conv_b, w_down, ln2_g, ln2_b):
    return _forward(x, w_in, diff_lambda, diff_norm_g, sink_logits, w_branch, w_o, ln1_g, ln1_b,
                    w_up, conv_w, conv_b, w_down, ln2_g, ln2_b)
```

```python
import functools
import math

import numpy as np
import jax
import jax.numpy as jnp
from jax import lax
from jax.experimental import pallas as pl
from jax.experimental.pallas import tpu as pltpu

F32 = jnp.float32
BF16 = jnp.bfloat16

HEAD_DIM = 128
BLOCK = 128
SCALE = HEAD_DIM ** -0.5
A_HEADS = 6
A_V_DIM = 2 * HEAD_DIM
B_Q_HEADS = 8
B_KV_HEADS = 2
B_WINDOW = 128
C_GROUPS = ((128, 1), (512, 4), (2048, 16))
C_HEADS_PER_GROUP = 4
C_HEADS = C_HEADS_PER_GROUP * len(C_GROUPS)
N_ALIBI_HEADS = B_Q_HEADS + C_HEADS + A_HEADS
A_WIDTH = A_HEADS * A_V_DIM
B_WIDTH = B_Q_HEADS * HEAD_DIM
C_WIDTH = C_HEADS * HEAD_DIM
CONV_WIDTH = 3
LN_EPS = 1e-5
RMS_EPS = 1e-5
LOG2E = math.log2(math.e)
LN2 = math.log(2.0)
NEG_BIG = -1e30

OFF_AQ = 0
OFF_AK = OFF_AQ + A_WIDTH
OFF_AV = OFF_AK + A_WIDTH
OFF_BQ = OFF_AV + A_WIDTH
OFF_BK = OFF_BQ + B_WIDTH
OFF_BV = OFF_BK + B_KV_HEADS * HEAD_DIM
OFF_CQ = OFF_BV + B_KV_HEADS * HEAD_DIM
OFF_CK = OFF_CQ + C_WIDTH
OFF_CV = OFF_CK + C_WIDTH
QKV_WIDTH = OFF_CV + C_WIDTH

V7X_VMEM_LIMIT_BYTES = 60 * 1024 * 1024
ROW_CHUNK = 256


def _params(n_axes, vmem=V7X_VMEM_LIMIT_BYTES):
    return pltpu.CompilerParams(dimension_semantics=("arbitrary",) * n_axes, vmem_limit_bytes=vmem)


def _cast_kernel(x_ref, o_ref):
    o_ref[...] = x_ref[...].astype(o_ref.dtype)


def cast_bf16(x, rows):
    r, c = x.shape
    return pl.pallas_call(
        _cast_kernel,
        grid=(r // rows,),
        in_specs=[pl.BlockSpec((rows, c), lambda i: (i, 0))],
        out_specs=pl.BlockSpec((rows, c), lambda i: (i, 0)),
        out_shape=jax.ShapeDtypeStruct((r, c), BF16),
        compiler_params=_params(1),
        name="cast_bf16",
    )(x)


def _cast_weight(w_ref, wb_ref):
    def body(i, c):
        r = pl.multiple_of(i * ROW_CHUNK, ROW_CHUNK)
        wb_ref[pl.ds(r, ROW_CHUNK), :] = w_ref[pl.ds(r, ROW_CHUNK), :].astype(BF16)
        return c

    lax.fori_loop(0, w_ref.shape[0] // ROW_CHUNK, body, 0)


def _inproj_kernel(x_ref, w_ref, *refs, has_scale, pair_widths):
    n_side = len(pair_widths)
    cs_ref = refs[0] if has_scale else None
    refs = refs[1:] if has_scale else refs
    side_in, o_ref, side_out, wb_ref = refs[:n_side], refs[n_side], refs[n_side + 1:-1], refs[-1]
    for s_in, s_out, pw in zip(side_in, side_out, pair_widths):
        if pw is None:
            s_out[...] = s_in[...].astype(s_out.dtype)
        else:
            half = s_in.shape[1] // 2
            for j in range(half // pw):
                s_out[:, 2 * j * pw:(2 * j + 1) * pw] = s_in[:, j * pw:(j + 1) * pw].astype(s_out.dtype)
                s_out[:, (2 * j + 1) * pw:(2 * j + 2) * pw] = (
                    s_in[:, half + j * pw:half + (j + 1) * pw].astype(s_out.dtype))

    @pl.when(pl.program_id(1) == 0)
    def _():
        _cast_weight(w_ref, wb_ref)

    for c in range(x_ref.shape[0] // ROW_CHUNK):
        rows = slice(c * ROW_CHUNK, (c + 1) * ROW_CHUNK)
        acc = jnp.dot(x_ref[rows, :], wb_ref[...], preferred_element_type=F32)
        acc = acc * cs_ref[...] if has_scale else jax.nn.sigmoid(acc)
        o_ref[rows, :] = acc.astype(o_ref.dtype)


def in_projection(xb, w, layer, col0, width, *, colscale, out_dtype, tm, tn, side=(), name):
    m, k = xb.shape
    assert col0 % tn == 0 and width % tn == 0 and m % tm == 0
    cb = col0 // tn
    nm = m // tm
    steps = (width // tn) * nm
    in_specs = [
        pl.BlockSpec((tm, k), lambda n, i: (i, 0)),
        pl.BlockSpec((None, k, tn), lambda n, i: (layer, 0, cb + n)),
    ]
    args = [xb, w]
    out_specs = [pl.BlockSpec((tm, tn), lambda n, i: (i, n))]
    out_shapes = [jax.ShapeDtypeStruct((m, width), out_dtype)]
    if colscale is not None:
        in_specs.append(pl.BlockSpec((1, tn), lambda n, i: (0, n)))
        args.append(colscale)
    for arr, _ in side:
        r, c = arr.shape[-2:]
        slab = r // steps
        assert slab * steps == r and slab % 16 == 0
        in_specs.append(pl.BlockSpec((None, slab, c), lambda n, i: (layer, n * nm + i, 0)))
        out_specs.append(pl.BlockSpec((slab, c), lambda n, i: (n * nm + i, 0)))
        out_shapes.append(jax.ShapeDtypeStruct((r, c), BF16))
        args.append(arr)
    res = pl.pallas_call(
        functools.partial(_inproj_kernel, has_scale=colscale is not None,
                          pair_widths=tuple(pw for _, pw in side)),
        grid=(width // tn, nm),
        in_specs=in_specs,
        out_specs=out_specs,
        out_shape=out_shapes,
        scratch_shapes=[pltpu.VMEM((k, tn), BF16)],
        compiler_params=_params(2),
        name=name,
    )(*args)
    return res if side else res[0]


def _lane_tile(x, n):
    return jnp.concatenate([x] * n, axis=1)


def _attn_a_kernel(slopes_ref, q_ref, k_ref, v_ref, lp_ref, g_ref, o_ref, m_ref, l_ref, acc_ref,
                   *, tq, tk, lam_init, slope0):
    h = pl.program_id(1)
    qi = pl.program_id(2)
    slope = slopes_ref[slope0 + h]
    col = lax.broadcasted_iota(jnp.int32, (1, tk), 1)
    nt = (((1,), (1,)), ((), ()))

    m_ref[...] = jnp.full(m_ref.shape, NEG_BIG, F32)
    l_ref[...] = jnp.zeros(l_ref.shape, F32)
    acc_ref[...] = jnp.zeros(acc_ref.shape, F32)

    def step(r0, js, rel, masked):
        rows = slice(r0, tq)
        vj = v_ref[pl.ds(js, tk), :]
        cb = slope * (col + rel).astype(F32)
        for mi in range(2):
            hd = slice(mi * HEAD_DIM, (mi + 1) * HEAD_DIM)
            s = lax.dot_general(q_ref[rows, hd], k_ref[pl.ds(js, tk), hd], nt, preferred_element_type=F32) + cb
            if masked:
                keep = (lax.broadcasted_iota(jnp.int32, (tq - r0, tk), 0)
                        >= lax.broadcasted_iota(jnp.int32, (tq - r0, tk), 1))
                s = jnp.where(keep, s, NEG_BIG)
            mo = m_ref[mi, rows]
            mn = jnp.maximum(mo, jnp.max(s, axis=-1, keepdims=True))
            a = jnp.exp2(mo - mn)
            p = jnp.exp2(s - _lane_tile(mn, tk // HEAD_DIM))
            l_ref[mi, rows] = a * l_ref[mi, rows] + jnp.sum(p, axis=-1, keepdims=True)
            m_ref[mi, rows] = mn
            acc_ref[mi, rows] = (_lane_tile(a, A_V_DIM // HEAD_DIM) * acc_ref[mi, rows]
                                 + jnp.dot(p.astype(BF16), vj, preferred_element_type=F32))

    def body(j, c):
        js = pl.multiple_of(j * tk, tk)
        step(0, js, js - qi * tq, False)
        return c

    lax.fori_loop(0, qi * (tq // tk), body, 0)
    for jj in range(tq // tk):
        step(jj * tk, pl.multiple_of(qi * tq + jj * tk, tk), jj * tk, True)

    lp = lp_ref[...]
    lam = (jnp.exp(jnp.sum(lp[0:1] * lp[1:2], axis=-1, keepdims=True))
           - jnp.exp(jnp.sum(lp[2:3] * lp[3:4], axis=-1, keepdims=True)) + lam_init)
    rep = A_V_DIM // HEAD_DIM
    o = (acc_ref[0] * _lane_tile(1.0 / l_ref[0], rep)
         - lam * (acc_ref[1] * _lane_tile(1.0 / l_ref[1], rep)))
    o = o * lax.rsqrt(jnp.mean(jnp.square(o), axis=-1, keepdims=True) + RMS_EPS) * g_ref[...]
    o_ref[...] = (o * (1.0 - lam_init)).astype(o_ref.dtype)


def diff_attention(qkv, slopes2, lam_params, norm_g, *, batch, seq, lam_init, tq, tk):
    nq = seq // tq
    kb = OFF_AK // A_V_DIM
    vb = OFF_AV // A_V_DIM
    assert tq % tk == 0 and seq % tq == 0
    return pl.pallas_call(
        functools.partial(_attn_a_kernel, tq=tq, tk=tk, lam_init=lam_init, slope0=B_Q_HEADS + C_HEADS),
        grid=(batch, A_HEADS, nq),
        in_specs=[
            pl.BlockSpec(memory_space=pltpu.SMEM),
            pl.BlockSpec((tq, A_V_DIM), lambda b, h, i: (b * nq + i, h)),
            pl.BlockSpec((seq, A_V_DIM), lambda b, h, i: (b, kb + h)),
            pl.BlockSpec((seq, A_V_DIM), lambda b, h, i: (b, vb + h)),
            pl.BlockSpec((4, HEAD_DIM), lambda b, h, i: (0, 0)),
            pl.BlockSpec((1, A_V_DIM), lambda b, h, i: (0, 0)),
        ],
        out_specs=pl.BlockSpec((tq, A_V_DIM), lambda b, h, i: (b * nq + i, h)),
        out_shape=jax.ShapeDtypeStruct((batch * seq, A_WIDTH), BF16),
        scratch_shapes=[pltpu.VMEM((2, tq, HEAD_DIM), F32), pltpu.VMEM((2, tq, HEAD_DIM), F32),
                        pltpu.VMEM((2, tq, A_V_DIM), F32)],
        compiler_params=_params(3),
        name="diff_attention",
    )(slopes2, qkv, qkv, qkv, lam_params, norm_g.reshape(1, A_V_DIM))


def _banded_kernel(*refs, dil, tq, hk, g, max_dist, slope0, has_sink, want_lse, whole_seq):
    it = iter(refs)
    slopes_ref = next(it)
    sinks_ref = next(it) if has_sink else None
    q_ref, kp_ref, k_ref, vp_ref, v_ref = (next(it) for _ in range(5))
    o_ref = next(it)
    lse_ref = next(it) if want_lse else None

    khs = pl.program_id(1)
    qi = pl.program_id(2)
    nt = (((1,), (1,)), ((), ()))

    def rows_of(blk, r):
        if dil == 1:
            return slice(blk * BLOCK, (blk + 1) * BLOCK)
        return pl.ds(blk * BLOCK * dil + r, BLOCK, stride=dil)

    def residue(r):
        row = lax.broadcasted_iota(jnp.int32, (BLOCK, 2 * BLOCK), 0)
        col = lax.broadcasted_iota(jnp.int32, (BLOCK, 2 * BLOCK), 1)
        dist = row + BLOCK - col
        band = (dist >= 0) & (dist <= max_dist)
        first_key = jnp.where(qi > 0, 0, BLOCK)
        band_first = band & (col >= first_key)
        dist_f = (dist * dil).astype(F32)
        for sb in range(tq // BLOCK):
            rows = rows_of(sb, r)
            valid = band_first if sb == 0 else band
            for kh in range(hk):
                kc = slice(kh * HEAD_DIM, (kh + 1) * HEAD_DIM)
                if sb == 0 and whole_seq:
                    kprev = vprev = jnp.zeros((BLOCK, HEAD_DIM), k_ref.dtype)
                elif sb == 0:
                    kprev, vprev = kp_ref[rows_of(0, r), kc], vp_ref[rows_of(0, r), kc]
                else:
                    kprev, vprev = k_ref[rows_of(sb - 1, r), kc], v_ref[rows_of(sb - 1, r), kc]
                kk = jnp.concatenate([kprev, k_ref[rows, kc]], axis=0).astype(BF16)
                vv = jnp.concatenate([vprev, v_ref[rows, kc]], axis=0).astype(BF16)
                hc = [slice((kh * g + gi) * HEAD_DIM, (kh * g + gi + 1) * HEAD_DIM) for gi in range(g)]
                qs = jnp.concatenate([q_ref[rows, c] for c in hc], axis=0).astype(BF16)
                s_all = lax.dot_general(qs, kk, nt, preferred_element_type=F32)
                es, invs = [], []
                for gi in range(g):
                    head = (khs * hk + kh) * g + gi
                    slope = slopes_ref[slope0 + head]
                    s = s_all[gi * BLOCK:(gi + 1) * BLOCK] - slope * dist_f
                    s = jnp.where(valid, s, NEG_BIG)
                    m = jnp.max(s, axis=-1, keepdims=True)
                    if has_sink:
                        sk = sinks_ref[head]
                        m = jnp.maximum(m, sk)
                    e = jnp.exp2(s - m)
                    den = jnp.sum(e, axis=-1, keepdims=True)
                    if has_sink:
                        den = den + jnp.exp2(sk - m)
                    es.append(e.astype(BF16))
                    invs.append(1.0 / den)
                    if want_lse:
                        lse = (m + jnp.log2(den)) * LN2
                        lse_ref[rows, hc[gi]] = jnp.broadcast_to(lse, (BLOCK, HEAD_DIM))
                o_all = jnp.dot(jnp.concatenate(es, axis=0), vv, preferred_element_type=F32)
                for gi in range(g):
                    o = o_all[gi * BLOCK:(gi + 1) * BLOCK] * invs[gi]
                    o_ref[rows, hc[gi]] = o.astype(o_ref.dtype)

    if dil == 1:
        residue(0)
    else:
        def body(r, c):
            residue(r)
            return c

        lax.fori_loop(0, dil, body, 0)


def banded_attention(qkv, slopes2, sinks2, *, batch, seq, dil, tq, hk, g, n_steps, q_off, k_off, v_off,
                     max_dist, slope0, out_width, out_dtype, want_lse):
    tokens = qkv.shape[0]
    tt = tq * dil
    pt = BLOCK * dil
    nq = seq // tt
    qw = hk * g * HEAD_DIM
    kw = hk * HEAD_DIM
    qcb, kcb, vcb = q_off // qw, k_off // kw, v_off // kw
    assert q_off % qw == 0 and k_off % kw == 0 and v_off % kw == 0 and seq % tt == 0 and tt % pt == 0

    def prev_row(b, i):
        return jnp.maximum(b * (seq // pt) + i * (tt // pt) - 1, 0)

    has_sink = sinks2 is not None
    in_specs = [pl.BlockSpec(memory_space=pltpu.SMEM)]
    args = [slopes2]
    if has_sink:
        in_specs.append(pl.BlockSpec(memory_space=pltpu.SMEM))
        args.append(sinks2)
    in_specs += [
        pl.BlockSpec((tt, qw), lambda b, s, i: (b * nq + i, qcb + s)),
        pl.BlockSpec((pt, kw), lambda b, s, i: (prev_row(b, i), kcb + s)),
        pl.BlockSpec((tt, kw), lambda b, s, i: (b * nq + i, kcb + s)),
        pl.BlockSpec((pt, kw), lambda b, s, i: (prev_row(b, i), vcb + s)),
        pl.BlockSpec((tt, kw), lambda b, s, i: (b * nq + i, vcb + s)),
    ]
    args += [qkv] * 5
    out_spec = pl.BlockSpec((tt, qw), lambda b, s, i: (b * nq + i, s))
    out_shapes = [jax.ShapeDtypeStruct((tokens, out_width), out_dtype)]
    out_specs = [out_spec]
    if want_lse:
        out_shapes.append(jax.ShapeDtypeStruct((tokens, out_width), F32))
        out_specs.append(out_spec)
    return pl.pallas_call(
        functools.partial(_banded_kernel, dil=dil, tq=tq, hk=hk, g=g, max_dist=max_dist, slope0=slope0,
                          has_sink=has_sink, want_lse=want_lse, whole_seq=nq == 1),
        grid=(batch, n_steps, nq),
        in_specs=in_specs,
        out_specs=out_specs,
        out_shape=out_shapes,
        compiler_params=_params(3),
        name=f"banded_attention_d{dil}_g{g}",
    )(*args)


def _merge_kernel(o0_ref, o1_ref, o2_ref, l0_ref, l1_ref, l2_ref, out_ref):
    l0, l1, l2 = l0_ref[...], l1_ref[...], l2_ref[...]
    mx = jnp.maximum(jnp.maximum(l0, l1), l2)
    e0, e1, e2 = jnp.exp(l0 - mx), jnp.exp(l1 - mx), jnp.exp(l2 - mx)
    inv = 1.0 / (e0 + e1 + e2)
    gw = o0_ref.shape[1]
    out_ref[:, 0:gw] = (e0 * inv * o0_ref[...]).astype(out_ref.dtype)
    out_ref[:, gw:2 * gw] = (e1 * inv * o1_ref[...]).astype(out_ref.dtype)
    out_ref[:, 2 * gw:3 * gw] = (e2 * inv * o2_ref[...]).astype(out_ref.dtype)


def merge_groups(outs, lses, tm):
    tokens, gw = outs[0].shape
    spec = pl.BlockSpec((tm, gw), lambda i: (i, 0))
    return pl.pallas_call(
        _merge_kernel,
        grid=(tokens // tm,),
        in_specs=[spec] * 6,
        out_specs=pl.BlockSpec((tm, 3 * gw), lambda i: (i, 0)),
        out_shape=jax.ShapeDtypeStruct((tokens, 3 * gw), BF16),
        compiler_params=_params(1),
        name="merge_dilation_groups",
    )(*outs, *lses)


def _branch_kernel(oa_ref, ob_ref, oc_ref, ga_ref, gb_ref, gc_ref, w_ref, y_ref):
    tm = oa_ref.shape[0]
    ka, kb = oa_ref.shape[1], ob_ref.shape[1]
    for c in range(tm // ROW_CHUNK):
        rows = slice(c * ROW_CHUNK, (c + 1) * ROW_CHUNK)
        ya = jnp.dot(oa_ref[rows, :], w_ref[0:ka, :], preferred_element_type=F32)
        yb = jnp.dot(ob_ref[rows, :], w_ref[ka:ka + kb, :], preferred_element_type=F32)
        yc = jnp.dot(oc_ref[rows, :], w_ref[ka + kb:, :], preferred_element_type=F32)
        y = (ga_ref[rows, :].astype(F32) * ya + gb_ref[rows, :].astype(F32) * yb
             + gc_ref[rows, :].astype(F32) * yc)
        y_ref[rows, :] = y.astype(y_ref.dtype)


def branch_projection(o_a, o_b, o_c, gates, wb, *, tm, tn):
    m = o_a.shape[0]
    kmix, d = wb.shape
    nb = d // tn
    return pl.pallas_call(
        _branch_kernel,
        grid=(m // tm, nb),
        in_specs=[
            pl.BlockSpec((tm, o_a.shape[1]), lambda i, n: (i, 0)),
            pl.BlockSpec((tm, o_b.shape[1]), lambda i, n: (i, 0)),
            pl.BlockSpec((tm, o_c.shape[1]), lambda i, n: (i, 0)),
            pl.BlockSpec((tm, tn), lambda i, n: (i, n)),
            pl.BlockSpec((tm, tn), lambda i, n: (i, nb + n)),
            pl.BlockSpec((tm, tn), lambda i, n: (i, 2 * nb + n)),
            pl.BlockSpec((kmix, tn), lambda i, n: (0, n)),
        ],
        out_specs=pl.BlockSpec((tm, tn), lambda i, n: (i, n)),
        out_shape=jax.ShapeDtypeStruct((m, d), BF16),
        compiler_params=_params(2),
        name="branch_projection",
    )(o_a, o_b, o_c, gates, gates, gates, wb)


def _resid_mm_bf16_kernel(a_ref, w_ref, *refs, alpha):
    *resid, o_ref = refs
    tm = a_ref.shape[0]
    for c in range(tm // ROW_CHUNK):
        rows = slice(c * ROW_CHUNK, (c + 1) * ROW_CHUNK)
        acc = jnp.dot(a_ref[rows, :], w_ref[...], preferred_element_type=F32)
        o_ref[rows, :] = alpha * _residual_rows(resid, rows) + acc


def residual_matmul_bf16(a, wb, resid, *, alpha, tm, tn, name):
    m, k = a.shape
    d = wb.shape[1]
    return pl.pallas_call(
        functools.partial(_resid_mm_bf16_kernel, alpha=alpha),
        grid=(m // tm, d // tn),
        in_specs=[
            pl.BlockSpec((tm, k), lambda i, n: (i, 0)),
            pl.BlockSpec((k, tn), lambda i, n: (0, n)),
        ] + _residual_specs(resid, tm, tn, lambda i, n: i, lambda i, n: n),
        out_specs=pl.BlockSpec((tm, tn), lambda i, n: (i, n)),
        out_shape=jax.ShapeDtypeStruct((m, d), F32),
        compiler_params=_params(2),
        name=name,
    )(a, wb, *resid)


def _layer_norm_rows(x, mu, rstd, g, b):
    return (x - mu) * rstd * g + b


def _residual_rows(resid, rows):
    if len(resid) == 1:
        return resid[0][rows, :]
    pre_ref, mu_ref, rs_ref, g_ref, b_ref = resid
    n = pre_ref.shape[1] // mu_ref.shape[1]
    return _layer_norm_rows(pre_ref[rows, :], _lane_tile(mu_ref[rows, :], n), _lane_tile(rs_ref[rows, :], n),
                            g_ref[...], b_ref[...])


def _residual_specs(resid, tm, tn, row_of, col_of):
    tile = pl.BlockSpec((tm, tn), lambda *a: (row_of(*a), col_of(*a)))
    if len(resid) == 1:
        return [tile]
    stat = pl.BlockSpec((tm, HEAD_DIM), lambda *a: (row_of(*a), 0))
    vec = pl.BlockSpec((1, tn), lambda *a: (0, col_of(*a)))
    return [tile, stat, stat, vec, vec]


def _ln_stats_kernel(x_ref, g_ref, b_ref, ob_ref, mu_ref, rs_ref):
    x = x_ref[...]
    mu = jnp.mean(x, axis=-1, keepdims=True)
    rstd = lax.rsqrt(jnp.mean(jnp.square(x - mu), axis=-1, keepdims=True) + LN_EPS)
    ob_ref[...] = _layer_norm_rows(x, mu, rstd, g_ref[...], b_ref[...]).astype(ob_ref.dtype)
    mu_ref[...] = jnp.broadcast_to(mu, mu_ref.shape)
    rs_ref[...] = jnp.broadcast_to(rstd, rs_ref.shape)


def _ln_kernel(x_ref, g_ref, b_ref, o_ref):
    x = x_ref[...]
    mu = jnp.mean(x, axis=-1, keepdims=True)
    rstd = lax.rsqrt(jnp.mean(jnp.square(x - mu), axis=-1, keepdims=True) + LN_EPS)
    o_ref[...] = _layer_norm_rows(x, mu, rstd, g_ref[...], b_ref[...])


def layer_norm(x, g, b, *, tm, last):
    m, d = x.shape
    spec = pl.BlockSpec((tm, d), lambda i: (i, 0))
    vec = pl.BlockSpec((1, d), lambda i: (0, 0))
    stat = pl.BlockSpec((tm, HEAD_DIM), lambda i: (i, 0))
    g, b = g.reshape(1, d), b.reshape(1, d)
    if last:
        return pl.pallas_call(
            _ln_kernel, grid=(m // tm,), in_specs=[spec, vec, vec], out_specs=spec,
            out_shape=jax.ShapeDtypeStruct((m, d), F32), compiler_params=_params(1), name="layer_norm_out",
        )(x, g, b)
    xb, mu, rstd = pl.pallas_call(
        _ln_stats_kernel,
        grid=(m // tm,),
        in_specs=[spec, vec, vec],
        out_specs=[spec, stat, stat],
        out_shape=[jax.ShapeDtypeStruct((m, d), BF16), jax.ShapeDtypeStruct((m, HEAD_DIM), F32),
                   jax.ShapeDtypeStruct((m, HEAD_DIM), F32)],
        compiler_params=_params(1),
        name="layer_norm",
    )(x, g, b)
    return xb, (x, mu, rstd, g, b)


LN_FUSED_ROWS = 64


def _resid_mm_ln_kernel(a_ref, w_ref, *refs, alpha):
    *resid, g_ref, b_ref, pre_ref, xb_ref, mu_ref, rs_ref = refs
    n = pl.program_id(1)
    tm, tn = a_ref.shape[0], w_ref.shape[1]
    col = pl.multiple_of(n * tn, tn)
    for c in range(tm // ROW_CHUNK):
        rows = slice(c * ROW_CHUNK, (c + 1) * ROW_CHUNK)
        acc = jnp.dot(a_ref[rows, :], w_ref[...], preferred_element_type=F32)
        pre_ref[rows, pl.ds(col, tn)] = alpha * _residual_rows(resid, rows) + acc

    @pl.when(n == pl.num_programs(1) - 1)
    def _():
        def body(i, c):
            rows = pl.ds(pl.multiple_of(i * LN_FUSED_ROWS, LN_FUSED_ROWS), LN_FUSED_ROWS)
            x = pre_ref[rows, :]
            mu = jnp.mean(x, axis=-1, keepdims=True)
            rstd = lax.rsqrt(jnp.mean(jnp.square(x - mu), axis=-1, keepdims=True) + LN_EPS)
            xb_ref[rows, :] = _layer_norm_rows(x, mu, rstd, g_ref[...], b_ref[...]).astype(xb_ref.dtype)
            mu_ref[rows, :] = jnp.broadcast_to(mu, (LN_FUSED_ROWS, HEAD_DIM))
            rs_ref[rows, :] = jnp.broadcast_to(rstd, (LN_FUSED_ROWS, HEAD_DIM))
            return c

        lax.fori_loop(0, tm // LN_FUSED_ROWS, body, 0)


def residual_matmul_ln(a, wb, resid, g, b, *, alpha, tm, tn, name):
    m, k = a.shape
    d = wb.shape[1]
    g, b = g.reshape(1, d), b.reshape(1, d)
    full = pl.BlockSpec((tm, d), lambda i, n: (i, 0))
    stat = pl.BlockSpec((tm, HEAD_DIM), lambda i, n: (i, 0))
    vec = pl.BlockSpec((1, d), lambda i, n: (0, 0))
    pre, xb, mu, rstd = pl.pallas_call(
        functools.partial(_resid_mm_ln_kernel, alpha=alpha),
        grid=(m // tm, d // tn),
        in_specs=[
            pl.BlockSpec((tm, k), lambda i, n: (i, 0)),
            pl.BlockSpec((k, tn), lambda i, n: (0, n)),
        ] + _residual_specs(resid, tm, tn, lambda i, n: i, lambda i, n: n) + [vec, vec],
        out_specs=[full, full, stat, stat],
        out_shape=[jax.ShapeDtypeStruct((m, d), F32), jax.ShapeDtypeStruct((m, d), BF16),
                   jax.ShapeDtypeStruct((m, HEAD_DIM), F32), jax.ShapeDtypeStruct((m, HEAD_DIM), F32)],
        compiler_params=_params(2),
        name=name,
    )(a, wb, *resid, g, b)
    return xb, (pre, mu, rstd, g, b)


def _up_kernel(x_ref, w_ref, cwg_ref, cwv_ref, cbg_ref, cbv_ref, wd_ref, o_ref, wdb_ref, halo_ref, *, seq):
    i, n = pl.program_id(0), pl.program_id(1)
    tm, tn = o_ref.shape
    wdb_ref[...] = wd_ref[...].astype(wdb_ref.dtype)

    @pl.when(lax.rem(i * tm, seq) == 0)
    def _():
        halo_ref[n] = jnp.zeros(halo_ref.shape[1:], F32)

    row = lax.broadcasted_iota(jnp.int32, (ROW_CHUNK, tn), 0)
    halo = halo_ref[n]
    for c in range(tm // ROW_CHUNK):
        rows = slice(c * ROW_CHUNK, (c + 1) * ROW_CHUNK)
        h = jnp.dot(x_ref[rows, :], w_ref[...], preferred_element_type=F32)
        prev, halo = halo, h[ROW_CHUNK - 8:, :]
        acts = []
        for half, (cw_ref, cb_ref) in enumerate(((cwg_ref, cbg_ref), (cwv_ref, cbv_ref))):
            hh = h[:, half * tn:(half + 1) * tn]
            hl = prev[:, half * tn:(half + 1) * tn]
            h1 = jnp.where(row == 0, hl[7:8], pltpu.roll(hh, 1, 0))
            h2 = jnp.where(row == 0, hl[6:7], jnp.where(row == 1, hl[7:8], pltpu.roll(hh, 2, 0)))
            cw = cw_ref[...]
            acts.append(cb_ref[...] + cw[0:1] * h2 + cw[1:2] * h1 + cw[2:3] * hh)
        gate, val = acts
        o_ref[rows, :] = (gate * jax.nn.sigmoid(gate) * val).astype(o_ref.dtype)
    halo_ref[n] = halo


def up_conv_glu(xb, wub, conv_w, conv_b, w_down, layer, *, seq, tm, tn):
    m, k = xb.shape
    dff = wub.shape[1] // 2
    nb = dff // tn
    nm = m // tm
    assert seq % tm == 0 and dff % tn == 0
    kd, dd = w_down.shape[-2:]
    slab = kd // (nb * nm)
    assert slab * nb * nm == kd and slab % 16 == 0
    conv_b = conv_b.reshape(conv_b.shape[0], 1, 2 * dff)
    return pl.pallas_call(
        functools.partial(_up_kernel, seq=seq),
        grid=(nm, nb),
        in_specs=[
            pl.BlockSpec((tm, k), lambda i, n: (i, 0)),
            pl.BlockSpec((k, 2 * tn), lambda i, n: (0, n)),
            pl.BlockSpec((None, CONV_WIDTH, tn), lambda i, n: (layer, 0, n)),
            pl.BlockSpec((None, CONV_WIDTH, tn), lambda i, n: (layer, 0, nb + n)),
            pl.BlockSpec((None, 1, tn), lambda i, n: (layer, 0, n)),
            pl.BlockSpec((None, 1, tn), lambda i, n: (layer, 0, nb + n)),
            pl.BlockSpec((None, slab, dd), lambda i, n: (layer, i * nb + n, 0)),
        ],
        out_specs=[pl.BlockSpec((tm, tn), lambda i, n: (i, n)),
                   pl.BlockSpec((slab, dd), lambda i, n: (i * nb + n, 0))],
        out_shape=[jax.ShapeDtypeStruct((m, dff), BF16), jax.ShapeDtypeStruct((kd, dd), BF16)],
        scratch_shapes=[pltpu.VMEM((nb, 8, 2 * tn), F32)],
        compiler_params=_params(2),
        name="up_conv_glu",
    )(xb, wub, conv_w, conv_w, conv_b, conv_b, w_down)


class _Tiles:
    mm_rows = 1024
    qkv_cols = 768
    qkv_c_cols = 768
    gate_cols = 768
    branch_cols = 1024
    out_cols = 1024
    out_ln_rows = 512
    up_rows = 2048
    up_cols = 256
    down_rows = 512
    down_cols = 512
    attn_a_q = 2048
    attn_a_k = 512
    banded_tokens = 512
    strided_tokens = 4096
    ln_rows = 512
    cast_rows = 512


def _forward(x, w_in, diff_lambda, diff_norm_g, sink_logits, w_branch, w_o, ln1_g, ln1_b,
             w_up, conv_w, conv_b, w_down, ln2_g, ln2_b, tiles=_Tiles):
    batch, seq, d = x.shape
    depth = w_in.shape[0]
    tokens = batch * seq
    alpha = (2 * depth) ** 0.25

    slopes = jnp.exp2(-8.0 * (jnp.arange(N_ALIBI_HEADS, dtype=F32) + 1.0) / N_ALIBI_HEADS)
    slopes2 = slopes * LOG2E
    colscale = np.ones((1, QKV_WIDTH), np.float32)
    for off, wdt in ((OFF_AQ, A_WIDTH), (OFF_BQ, B_WIDTH), (OFF_CQ, C_WIDTH)):
        colscale[:, off:off + wdt] = SCALE * LOG2E
    colscale = jnp.asarray(colscale)

    x32 = x.reshape(tokens, d)
    resid = (x32,)
    xb = cast_bf16(x32, tiles.cast_rows)
    for l in range(depth):
        lam_init = 0.8 - 0.6 * math.exp(-0.3 * l)
        qkv, wbb, wob = in_projection(xb, w_in, l, 0, OFF_CQ, colscale=colscale[:, :OFF_CQ], out_dtype=BF16,
                                      tm=tiles.mm_rows, tn=tiles.qkv_cols, name="inproj_qkv",
                                      side=((w_branch, None), (w_o, None)))
        qkv_c = in_projection(xb, w_in, l, OFF_CQ, 3 * C_WIDTH, colscale=colscale[:, OFF_CQ:], out_dtype=F32,
                              tm=tiles.mm_rows, tn=tiles.qkv_c_cols, name="inproj_qkv_c")
        gates, wub = in_projection(xb, w_in, l, QKV_WIDTH, 3 * d, colscale=None, out_dtype=BF16,
                                   tm=tiles.mm_rows, tn=tiles.gate_cols, name="inproj_gates",
                                   side=((w_up, tiles.up_cols),))
        o_a = diff_attention(qkv, slopes2, diff_lambda[l], diff_norm_g[l], batch=batch, seq=seq,
                             lam_init=lam_init, tq=tiles.attn_a_q, tk=tiles.attn_a_k)
        (o_b,) = banded_attention(
            qkv, slopes2, sink_logits[l] * LOG2E, batch=batch, seq=seq, dil=1, tq=tiles.banded_tokens,
            hk=1, g=B_Q_HEADS // B_KV_HEADS, n_steps=B_KV_HEADS, q_off=OFF_BQ, k_off=OFF_BK, v_off=OFF_BV,
            max_dist=B_WINDOW - 1, slope0=0, out_width=B_WIDTH, out_dtype=BF16, want_lse=False)
        c_outs, c_lses = [], []
        gw = C_HEADS_PER_GROUP * HEAD_DIM
        for gi, (window, dil) in enumerate(C_GROUPS):
            hk = C_HEADS_PER_GROUP if dil == 1 else 1
            o, lse = banded_attention(
                qkv_c, slopes2, None, batch=batch, seq=seq, dil=dil,
                tq=tiles.banded_tokens if dil == 1 else max(tiles.strided_tokens // dil, BLOCK),
                hk=hk, g=1, n_steps=C_HEADS_PER_GROUP // hk,
                q_off=gi * gw, k_off=C_WIDTH + gi * gw, v_off=2 * C_WIDTH + gi * gw,
                max_dist=window // dil, slope0=B_Q_HEADS + gi * C_HEADS_PER_GROUP,
                out_width=gw, out_dtype=F32, want_lse=True)
            c_outs.append(o)
            c_lses.append(lse)
        o_c = merge_groups(c_outs, c_lses, tiles.ln_rows)
        y = branch_projection(o_a, o_b, o_c, gates, wbb, tm=tiles.mm_rows, tn=tiles.branch_cols)
        xb, resid = residual_matmul_ln(y, wob, resid, ln1_g[l], ln1_b[l], alpha=alpha, tm=tiles.out_ln_rows,
                                       tn=tiles.out_cols, name="out_projection_ln")
        act, wdb = up_conv_glu(xb, wub, conv_w, conv_b, w_down, l, seq=seq, tm=tiles.up_rows, tn=tiles.up_cols)
        pre = residual_matmul_bf16(act, wdb, resid, alpha=alpha, tm=tiles.down_rows, tn=tiles.down_cols,
                                   name="down_projection")
        if l + 1 < depth:
            xb, resid = layer_norm(pre, ln2_g[l], ln2_b[l], tm=tiles.ln_rows, last=False)
    out = layer_norm(pre, ln2_g[depth - 1], ln2_b[depth - 1], tm=tiles.ln_rows, last=True)
    return out.reshape(batch, seq, d)


def kernel(x, w_in, diff_lambda, diff_norm_g, sink_logits, w_branch, w_o, ln1_g, ln1_b,
           w_up, conv_w, conv_b, w_down, ln2_g, ln2_b):
    return _forward(x, w_in, diff_lambda, diff_norm_g, sink_logits, w_branch, w_o, ln1_g, ln1_b,
                    w_up, conv_w, conv_b, w_down, ln2_g, ln2_b)
```

```python
import functools
import math

import numpy as np
import jax
import jax.numpy as jnp
from jax import lax
from jax.experimental import pallas as pl
from jax.experimental.pallas import tpu as pltpu

F32 = jnp.float32
BF16 = jnp.bfloat16

HEAD_DIM = 128
BLOCK = 128
SCALE = HEAD_DIM ** -0.5
A_HEADS = 6
A_V_DIM = 2 * HEAD_DIM
B_Q_HEADS = 8
B_KV_HEADS = 2
B_WINDOW = 128
C_GROUPS = ((128, 1), (512, 4), (2048, 16))
C_HEADS_PER_GROUP = 4
C_HEADS = C_HEADS_PER_GROUP * len(C_GROUPS)
N_ALIBI_HEADS = B_Q_HEADS + C_HEADS + A_HEADS
A_WIDTH = A_HEADS * A_V_DIM
B_WIDTH = B_Q_HEADS * HEAD_DIM
C_WIDTH = C_HEADS * HEAD_DIM
CONV_WIDTH = 3
LN_EPS = 1e-5
RMS_EPS = 1e-5
LOG2E = math.log2(math.e)
LN2 = math.log(2.0)
NEG_BIG = -1e30

OFF_AQ = 0
OFF_AK = OFF_AQ + A_WIDTH
OFF_AV = OFF_AK + A_WIDTH
OFF_BQ = OFF_AV + A_WIDTH
OFF_BK = OFF_BQ + B_WIDTH
OFF_BV = OFF_BK + B_KV_HEADS * HEAD_DIM
OFF_CQ = OFF_BV + B_KV_HEADS * HEAD_DIM
OFF_CK = OFF_CQ + C_WIDTH
OFF_CV = OFF_CK + C_WIDTH
QKV_WIDTH = OFF_CV + C_WIDTH

V7X_VMEM_LIMIT_BYTES = 60 * 1024 * 1024
ROW_CHUNK = 256


def _params(n_axes, vmem=V7X_VMEM_LIMIT_BYTES):
    return pltpu.CompilerParams(dimension_semantics=("arbitrary",) * n_axes, vmem_limit_bytes=vmem)


def _cast_kernel(x_ref, o_ref):
    o_ref[...] = x_ref[...].astype(o_ref.dtype)


def cast_bf16(x, rows):
    r, c = x.shape
    return pl.pallas_call(
        _cast_kernel,
        grid=(r // rows,),
        in_specs=[pl.BlockSpec((rows, c), lambda i: (i, 0))],
        out_specs=pl.BlockSpec((rows, c), lambda i: (i, 0)),
        out_shape=jax.ShapeDtypeStruct((r, c), BF16),
        compiler_params=_params(1),
        name="cast_bf16",
    )(x)


def _cast_weight(w_ref, wb_ref):
    def body(i, c):
        r = pl.multiple_of(i * ROW_CHUNK, ROW_CHUNK)
        wb_ref[pl.ds(r, ROW_CHUNK), :] = w_ref[pl.ds(r, ROW_CHUNK), :].astype(BF16)
        return c

    lax.fori_loop(0, w_ref.shape[0] // ROW_CHUNK, body, 0)


def _inproj_kernel(x_ref, w_ref, *refs, has_scale, pair_widths):
    n_side = len(pair_widths)
    cs_ref = refs[0] if has_scale else None
    refs = refs[1:] if has_scale else refs
    side_in, o_ref, side_out, wb_ref = refs[:n_side], refs[n_side], refs[n_side + 1:-1], refs[-1]
    for s_in, s_out, pw in zip(side_in, side_out, pair_widths):
        if pw is None:
            s_out[...] = s_in[...].astype(s_out.dtype)
        else:
            half = s_in.shape[1] // 2
            for j in range(half // pw):
                s_out[:, 2 * j * pw:(2 * j + 1) * pw] = s_in[:, j * pw:(j + 1) * pw].astype(s_out.dtype)
                s_out[:, (2 * j + 1) * pw:(2 * j + 2) * pw] = (
                    s_in[:, half + j * pw:half + (j + 1) * pw].astype(s_out.dtype))

    @pl.when(pl.program_id(1) == 0)
    def _():
        _cast_weight(w_ref, wb_ref)

    for c in range(x_ref.shape[0] // ROW_CHUNK):
        rows = slice(c * ROW_CHUNK, (c + 1) * ROW_CHUNK)
        acc = jnp.dot(x_ref[rows, :], wb_ref[...], preferred_element_type=F32)
        acc = acc * cs_ref[...] if has_scale else jax.nn.sigmoid(acc)
        o_ref[rows, :] = acc.astype(o_ref.dtype)


def in_projection(xb, w, layer, col0, width, *, colscale, out_dtype, tm, tn, side=(), name):
    m, k = xb.shape
    assert col0 % tn == 0 and width % tn == 0 and m % tm == 0
    cb = col0 // tn
    nm = m // tm
    steps = (width // tn) * nm
    in_specs = [
        pl.BlockSpec((tm, k), lambda n, i: (i, 0)),
        pl.BlockSpec((None, k, tn), lambda n, i: (layer, 0, cb + n)),
    ]
    args = [xb, w]
    out_specs = [pl.BlockSpec((tm, tn), lambda n, i: (i, n))]
    out_shapes = [jax.ShapeDtypeStruct((m, width), out_dtype)]
    if colscale is not None:
        in_specs.append(pl.BlockSpec((1, tn), lambda n, i: (0, n)))
        args.append(colscale)
    for arr, _ in side:
        r, c = arr.shape[-2:]
        slab = r // steps
        assert slab * steps == r and slab % 16 == 0
        in_specs.append(pl.BlockSpec((None, slab, c), lambda n, i: (layer, n * nm + i, 0)))
        out_specs.append(pl.BlockSpec((slab, c), lambda n, i: (n * nm + i, 0)))
        out_shapes.append(jax.ShapeDtypeStruct((r, c), BF16))
        args.append(arr)
    res = pl.pallas_call(
        functools.partial(_inproj_kernel, has_scale=colscale is not None,
                          pair_widths=tuple(pw for _, pw in side)),
        grid=(width // tn, nm),
        in_specs=in_specs,
        out_specs=out_specs,
        out_shape=out_shapes,
        scratch_shapes=[pltpu.VMEM((k, tn), BF16)],
        compiler_params=_params(2),
        name=name,
    )(*args)
    return res if side else res[0]


def _lane_tile(x, n):
    return jnp.concatenate([x] * n, axis=1)


def _attn_a_kernel(slopes_ref, q_ref, k_ref, v_ref, lp_ref, g_ref, o_ref, m_ref, l_ref, acc_ref,
                   *, tq, tk, lam_init, slope0):
    h = pl.program_id(1)
    qi = pl.program_id(2)
    slope = slopes_ref[slope0 + h]
    col = lax.broadcasted_iota(jnp.int32, (1, tk), 1)
    nt = (((1,), (1,)), ((), ()))

    m_ref[...] = jnp.full(m_ref.shape, NEG_BIG, F32)
    l_ref[...] = jnp.zeros(l_ref.shape, F32)
    acc_ref[...] = jnp.zeros(acc_ref.shape, F32)

    def step(r0, js, rel, masked):
        rows = slice(r0, tq)
        vj = v_ref[pl.ds(js, tk), :]
        cb = slope * (col + rel).astype(F32)
        half = tk // 2
        for mi in range(0 if masked else 2):
            hd = slice(mi * HEAD_DIM, (mi + 1) * HEAD_DIM)

            def scores(k0, width):
                return lax.dot_general(q_ref[rows, hd], k_ref[pl.ds(js + k0, width), hd], nt,
                                       preferred_element_type=F32) + cb[:, k0:k0 + width]

            mo = m_ref[mi, rows]
            mn = jnp.maximum(mo, jnp.max(scores(0, tk), axis=-1, keepdims=True))
            a = jnp.exp2(mo - mn)
            lsum = a * l_ref[mi, rows]
            pv = _lane_tile(a, A_V_DIM // HEAD_DIM) * acc_ref[mi, rows]
            for hf in range(2):
                p = jnp.exp2(scores(hf * half, half) - _lane_tile(mn, half // HEAD_DIM))
                lsum = lsum + jnp.sum(p, axis=-1, keepdims=True)
                pv = pv + jnp.dot(p.astype(BF16), vj[hf * half:(hf + 1) * half], preferred_element_type=F32)
            l_ref[mi, rows] = lsum
            m_ref[mi, rows] = mn
            acc_ref[mi, rows] = pv
        for mi in range(2 if masked else 0):
            hd = slice(mi * HEAD_DIM, (mi + 1) * HEAD_DIM)
            s = lax.dot_general(q_ref[rows, hd], k_ref[pl.ds(js, tk), hd], nt, preferred_element_type=F32) + cb
            if masked:
                keep = (lax.broadcasted_iota(jnp.int32, (tq - r0, tk), 0)
                        >= lax.broadcasted_iota(jnp.int32, (tq - r0, tk), 1))
                s = jnp.where(keep, s, NEG_BIG)
            mo = m_ref[mi, rows]
            mn = jnp.maximum(mo, jnp.max(s, axis=-1, keepdims=True))
            a = jnp.exp2(mo - mn)
            p = jnp.exp2(s - _lane_tile(mn, tk // HEAD_DIM))
            l_ref[mi, rows] = a * l_ref[mi, rows] + jnp.sum(p, axis=-1, keepdims=True)
            m_ref[mi, rows] = mn
            acc_ref[mi, rows] = (_lane_tile(a, A_V_DIM // HEAD_DIM) * acc_ref[mi, rows]
                                 + jnp.dot(p.astype(BF16), vj, preferred_element_type=F32))

    def body(j, c):
        js = pl.multiple_of(j * tk, tk)
        step(0, js, js - qi * tq, False)
        return c

    lax.fori_loop(0, qi * (tq // tk), body, 0)
    for jj in range(tq // tk):
        step(jj * tk, pl.multiple_of(qi * tq + jj * tk, tk), jj * tk, True)

    lp = lp_ref[...]
    lam = (jnp.exp(jnp.sum(lp[0:1] * lp[1:2], axis=-1, keepdims=True))
           - jnp.exp(jnp.sum(lp[2:3] * lp[3:4], axis=-1, keepdims=True)) + lam_init)
    rep = A_V_DIM // HEAD_DIM
    o = (acc_ref[0] * _lane_tile(1.0 / l_ref[0], rep)
         - lam * (acc_ref[1] * _lane_tile(1.0 / l_ref[1], rep)))
    o = o * lax.rsqrt(jnp.mean(jnp.square(o), axis=-1, keepdims=True) + RMS_EPS) * g_ref[...]
    o_ref[...] = (o * (1.0 - lam_init)).astype(o_ref.dtype)


def diff_attention(qkv, slopes2, lam_params, norm_g, *, batch, seq, lam_init, tq, tk):
    nq = seq // tq
    kb = OFF_AK // A_V_DIM
    vb = OFF_AV // A_V_DIM
    assert tq % tk == 0 and seq % tq == 0
    return pl.pallas_call(
        functools.partial(_attn_a_kernel, tq=tq, tk=tk, lam_init=lam_init, slope0=B_Q_HEADS + C_HEADS),
        grid=(batch, A_HEADS, nq),
        in_specs=[
            pl.BlockSpec(memory_space=pltpu.SMEM),
            pl.BlockSpec((tq, A_V_DIM), lambda b, h, i: (b * nq + i, h)),
            pl.BlockSpec((seq, A_V_DIM), lambda b, h, i: (b, kb + h)),
            pl.BlockSpec((seq, A_V_DIM), lambda b, h, i: (b, vb + h)),
            pl.BlockSpec((4, HEAD_DIM), lambda b, h, i: (0, 0)),
            pl.BlockSpec((1, A_V_DIM), lambda b, h, i: (0, 0)),
        ],
        out_specs=pl.BlockSpec((tq, A_V_DIM), lambda b, h, i: (b * nq + i, h)),
        out_shape=jax.ShapeDtypeStruct((batch * seq, A_WIDTH), BF16),
        scratch_shapes=[pltpu.VMEM((2, tq, HEAD_DIM), F32), pltpu.VMEM((2, tq, HEAD_DIM), F32),
                        pltpu.VMEM((2, tq, A_V_DIM), F32)],
        compiler_params=_params(3),
        name="diff_attention",
    )(slopes2, qkv, qkv, qkv, lam_params, norm_g.reshape(1, A_V_DIM))


def _banded_kernel(*refs, dil, tq, hk, g, max_dist, slope0, has_sink, want_lse, whole_seq):
    it = iter(refs)
    slopes_ref = next(it)
    sinks_ref = next(it) if has_sink else None
    q_ref, kp_ref, k_ref, vp_ref, v_ref = (next(it) for _ in range(5))
    o_ref = next(it)
    lse_ref = next(it) if want_lse else None

    khs = pl.program_id(1)
    qi = pl.program_id(2)
    nt = (((1,), (1,)), ((), ()))

    def rows_of(blk, r):
        if dil == 1:
            return slice(blk * BLOCK, (blk + 1) * BLOCK)
        return pl.ds(blk * BLOCK * dil + r, BLOCK, stride=dil)

    def residue(r):
        row = lax.broadcasted_iota(jnp.int32, (BLOCK, 2 * BLOCK), 0)
        col = lax.broadcasted_iota(jnp.int32, (BLOCK, 2 * BLOCK), 1)
        dist = row + BLOCK - col
        band = (dist >= 0) & (dist <= max_dist)
        first_key = jnp.where(qi > 0, 0, BLOCK)
        band_first = band & (col >= first_key)
        dist_f = (dist * dil).astype(F32)
        for sb in range(tq // BLOCK):
            rows = rows_of(sb, r)
            valid = band_first if sb == 0 else band
            for kh in range(hk):
                kc = slice(kh * HEAD_DIM, (kh + 1) * HEAD_DIM)
                if sb == 0 and whole_seq:
                    kprev = vprev = jnp.zeros((BLOCK, HEAD_DIM), k_ref.dtype)
                elif sb == 0:
                    kprev, vprev = kp_ref[rows_of(0, r), kc], vp_ref[rows_of(0, r), kc]
                else:
                    kprev, vprev = k_ref[rows_of(sb - 1, r), kc], v_ref[rows_of(sb - 1, r), kc]
                kk = jnp.concatenate([kprev, k_ref[rows, kc]], axis=0).astype(BF16)
                vv = jnp.concatenate([vprev, v_ref[rows, kc]], axis=0).astype(BF16)
                hc = [slice((kh * g + gi) * HEAD_DIM, (kh * g + gi + 1) * HEAD_DIM) for gi in range(g)]
                qs = jnp.concatenate([q_ref[rows, c] for c in hc], axis=0).astype(BF16)
                s_all = lax.dot_general(qs, kk, nt, preferred_element_type=F32)
                es, invs = [], []
                for gi in range(g):
                    head = (khs * hk + kh) * g + gi
                    slope = slopes_ref[slope0 + head]
                    s = s_all[gi * BLOCK:(gi + 1) * BLOCK] - slope * dist_f
                    s = jnp.where(valid, s, NEG_BIG)
                    m = jnp.max(s, axis=-1, keepdims=True)
                    if has_sink:
                        sk = sinks_ref[head]
                        m = jnp.maximum(m, sk)
                    e = jnp.exp2(s - m)
                    den = jnp.sum(e, axis=-1, keepdims=True)
                    if has_sink:
                        den = den + jnp.exp2(sk - m)
                    es.append(e.astype(BF16))
                    invs.append(1.0 / den)
                    if want_lse:
                        lse = (m + jnp.log2(den)) * LN2
                        lse_ref[rows, hc[gi]] = jnp.broadcast_to(lse, (BLOCK, HEAD_DIM))
                o_all = jnp.dot(jnp.concatenate(es, axis=0), vv, preferred_element_type=F32)
                for gi in range(g):
                    o = o_all[gi * BLOCK:(gi + 1) * BLOCK] * invs[gi]
                    o_ref[rows, hc[gi]] = o.astype(o_ref.dtype)

    if dil == 1:
        residue(0)
    else:
        def body(r, c):
            residue(r)
            return c

        lax.fori_loop(0, dil, body, 0)


def banded_attention(qkv, slopes2, sinks2, *, batch, seq, dil, tq, hk, g, n_steps, q_off, k_off, v_off,
                     max_dist, slope0, out_width, out_dtype, want_lse):
    tokens = qkv.shape[0]
    tt = tq * dil
    pt = BLOCK * dil
    nq = seq // tt
    qw = hk * g * HEAD_DIM
    kw = hk * HEAD_DIM
    qcb, kcb, vcb = q_off // qw, k_off // kw, v_off // kw
    assert q_off % qw == 0 and k_off % kw == 0 and v_off % kw == 0 and seq % tt == 0 and tt % pt == 0

    def prev_row(b, i):
        return jnp.maximum(b * (seq // pt) + i * (tt // pt) - 1, 0)

    has_sink = sinks2 is not None
    in_specs = [pl.BlockSpec(memory_space=pltpu.SMEM)]
    args = [slopes2]
    if has_sink:
        in_specs.append(pl.BlockSpec(memory_space=pltpu.SMEM))
        args.append(sinks2)
    in_specs += [
        pl.BlockSpec((tt, qw), lambda b, s, i: (b * nq + i, qcb + s)),
        pl.BlockSpec((pt, kw), lambda b, s, i: (prev_row(b, i), kcb + s)),
        pl.BlockSpec((tt, kw), lambda b, s, i: (b * nq + i, kcb + s)),
        pl.BlockSpec((pt, kw), lambda b, s, i: (prev_row(b, i), vcb + s)),
        pl.BlockSpec((tt, kw), lambda b, s, i: (b * nq + i, vcb + s)),
    ]
    args += [qkv] * 5
    out_spec = pl.BlockSpec((tt, qw), lambda b, s, i: (b * nq + i, s))
    out_shapes = [jax.ShapeDtypeStruct((tokens, out_width), out_dtype)]
    out_specs = [out_spec]
    if want_lse:
        out_shapes.append(jax.ShapeDtypeStruct((tokens, out_width), F32))
        out_specs.append(out_spec)
    return pl.pallas_call(
        functools.partial(_banded_kernel, dil=dil, tq=tq, hk=hk, g=g, max_dist=max_dist, slope0=slope0,
                          has_sink=has_sink, want_lse=want_lse, whole_seq=nq == 1),
        grid=(batch, n_steps, nq),
        in_specs=in_specs,
        out_specs=out_specs,
        out_shape=out_shapes,
        compiler_params=_params(3),
        name=f"banded_attention_d{dil}_g{g}",
    )(*args)


def _merge_kernel(o0_ref, o1_ref, o2_ref, l0_ref, l1_ref, l2_ref, out_ref):
    l0, l1, l2 = l0_ref[...], l1_ref[...], l2_ref[...]
    mx = jnp.maximum(jnp.maximum(l0, l1), l2)
    e0, e1, e2 = jnp.exp(l0 - mx), jnp.exp(l1 - mx), jnp.exp(l2 - mx)
    inv = 1.0 / (e0 + e1 + e2)
    gw = o0_ref.shape[1]
    out_ref[:, 0:gw] = (e0 * inv * o0_ref[...]).astype(out_ref.dtype)
    out_ref[:, gw:2 * gw] = (e1 * inv * o1_ref[...]).astype(out_ref.dtype)
    out_ref[:, 2 * gw:3 * gw] = (e2 * inv * o2_ref[...]).astype(out_ref.dtype)


def merge_groups(outs, lses, tm):
    tokens, gw = outs[0].shape
    spec = pl.BlockSpec((tm, gw), lambda i: (i, 0))
    return pl.pallas_call(
        _merge_kernel,
        grid=(tokens // tm,),
        in_specs=[spec] * 6,
        out_specs=pl.BlockSpec((tm, 3 * gw), lambda i: (i, 0)),
        out_shape=jax.ShapeDtypeStruct((tokens, 3 * gw), BF16),
        compiler_params=_params(1),
        name="merge_dilation_groups",
    )(*outs, *lses)


def _branch_kernel(oa_ref, ob_ref, oc_ref, ga_ref, gb_ref, gc_ref, w_ref, y_ref):
    tm = oa_ref.shape[0]
    ka, kb = oa_ref.shape[1], ob_ref.shape[1]
    for c in range(tm // ROW_CHUNK):
        rows = slice(c * ROW_CHUNK, (c + 1) * ROW_CHUNK)
        ya = jnp.dot(oa_ref[rows, :], w_ref[0:ka, :], preferred_element_type=F32)
        yb = jnp.dot(ob_ref[rows, :], w_ref[ka:ka + kb, :], preferred_element_type=F32)
        yc = jnp.dot(oc_ref[rows, :], w_ref[ka + kb:, :], preferred_element_type=F32)
        y = (ga_ref[rows, :].astype(F32) * ya + gb_ref[rows, :].astype(F32) * yb
             + gc_ref[rows, :].astype(F32) * yc)
        y_ref[rows, :] = y.astype(y_ref.dtype)


def branch_projection(o_a, o_b, o_c, gates, wb, *, tm, tn):
    m = o_a.shape[0]
    kmix, d = wb.shape
    nb = d // tn
    return pl.pallas_call(
        _branch_kernel,
        grid=(m // tm, nb),
        in_specs=[
            pl.BlockSpec((tm, o_a.shape[1]), lambda i, n: (i, 0)),
            pl.BlockSpec((tm, o_b.shape[1]), lambda i, n: (i, 0)),
            pl.BlockSpec((tm, o_c.shape[1]), lambda i, n: (i, 0)),
            pl.BlockSpec((tm, tn), lambda i, n: (i, n)),
            pl.BlockSpec((tm, tn), lambda i, n: (i, nb + n)),
            pl.BlockSpec((tm, tn), lambda i, n: (i, 2 * nb + n)),
            pl.BlockSpec((kmix, tn), lambda i, n: (0, n)),
        ],
        out_specs=pl.BlockSpec((tm, tn), lambda i, n: (i, n)),
        out_shape=jax.ShapeDtypeStruct((m, d), BF16),
        compiler_params=_params(2),
        name="branch_projection",
    )(o_a, o_b, o_c, gates, gates, gates, wb)


def _resid_mm_bf16_kernel(a_ref, w_ref, *refs, alpha):
    *resid, o_ref = refs
    tm = a_ref.shape[0]
    for c in range(tm // ROW_CHUNK):
        rows = slice(c * ROW_CHUNK, (c + 1) * ROW_CHUNK)
        acc = jnp.dot(a_ref[rows, :], w_ref[...], preferred_element_type=F32)
        o_ref[rows, :] = alpha * _residual_rows(resid, rows) + acc


def residual_matmul_bf16(a, wb, resid, *, alpha, tm, tn, name):
    m, k = a.shape
    d = wb.shape[1]
    return pl.pallas_call(
        functools.partial(_resid_mm_bf16_kernel, alpha=alpha),
        grid=(m // tm, d // tn),
        in_specs=[
            pl.BlockSpec((tm, k), lambda i, n: (i, 0)),
            pl.BlockSpec((k, tn), lambda i, n: (0, n)),
        ] + _residual_specs(resid, tm, tn, lambda i, n: i, lambda i, n: n),
        out_specs=pl.BlockSpec((tm, tn), lambda i, n: (i, n)),
        out_shape=jax.ShapeDtypeStruct((m, d), F32),
        compiler_params=_params(2),
        name=name,
    )(a, wb, *resid)


def _layer_norm_rows(x, mu, rstd, g, b):
    return (x - mu) * rstd * g + b


def _residual_rows(resid, rows):
    if len(resid) == 1:
        return resid[0][rows, :]
    pre_ref, mu_ref, rs_ref, g_ref, b_ref = resid
    n = pre_ref.shape[1] // mu_ref.shape[1]
    return _layer_norm_rows(pre_ref[rows, :], _lane_tile(mu_ref[rows, :], n), _lane_tile(rs_ref[rows, :], n),
                            g_ref[...], b_ref[...])


def _residual_specs(resid, tm, tn, row_of, col_of):
    tile = pl.BlockSpec((tm, tn), lambda *a: (row_of(*a), col_of(*a)))
    if len(resid) == 1:
        return [tile]
    stat = pl.BlockSpec((tm, HEAD_DIM), lambda *a: (row_of(*a), 0))
    vec = pl.BlockSpec((1, tn), lambda *a: (0, col_of(*a)))
    return [tile, stat, stat, vec, vec]


def _ln_stats_kernel(x_ref, g_ref, b_ref, ob_ref, mu_ref, rs_ref):
    x = x_ref[...]
    mu = jnp.mean(x, axis=-1, keepdims=True)
    rstd = lax.rsqrt(jnp.mean(jnp.square(x - mu), axis=-1, keepdims=True) + LN_EPS)
    ob_ref[...] = _layer_norm_rows(x, mu, rstd, g_ref[...], b_ref[...]).astype(ob_ref.dtype)
    mu_ref[...] = jnp.broadcast_to(mu, mu_ref.shape)
    rs_ref[...] = jnp.broadcast_to(rstd, rs_ref.shape)


def _ln_kernel(x_ref, g_ref, b_ref, o_ref):
    x = x_ref[...]
    mu = jnp.mean(x, axis=-1, keepdims=True)
    rstd = lax.rsqrt(jnp.mean(jnp.square(x - mu), axis=-1, keepdims=True) + LN_EPS)
    o_ref[...] = _layer_norm_rows(x, mu, rstd, g_ref[...], b_ref[...])


def layer_norm(x, g, b, *, tm, last):
    m, d = x.shape
    spec = pl.BlockSpec((tm, d), lambda i: (i, 0))
    vec = pl.BlockSpec((1, d), lambda i: (0, 0))
    stat = pl.BlockSpec((tm, HEAD_DIM), lambda i: (i, 0))
    g, b = g.reshape(1, d), b.reshape(1, d)
    if last:
        return pl.pallas_call(
            _ln_kernel, grid=(m // tm,), in_specs=[spec, vec, vec], out_specs=spec,
            out_shape=jax.ShapeDtypeStruct((m, d), F32), compiler_params=_params(1), name="layer_norm_out",
        )(x, g, b)
    xb, mu, rstd = pl.pallas_call(
        _ln_stats_kernel,
        grid=(m // tm,),
        in_specs=[spec, vec, vec],
        out_specs=[spec, stat, stat],
        out_shape=[jax.ShapeDtypeStruct((m, d), BF16), jax.ShapeDtypeStruct((m, HEAD_DIM), F32),
                   jax.ShapeDtypeStruct((m, HEAD_DIM), F32)],
        compiler_params=_params(1),
        name="layer_norm",
    )(x, g, b)
    return xb, (x, mu, rstd, g, b)


def _up_kernel(x_ref, w_ref, cwg_ref, cwv_ref, cbg_ref, cbv_ref, wd_ref, o_ref, wdb_ref, halo_ref, *, seq):
    i, n = pl.program_id(0), pl.program_id(1)
    tm, tn = o_ref.shape
    wdb_ref[...] = wd_ref[...].astype(wdb_ref.dtype)

    @pl.when(lax.rem(i * tm, seq) == 0)
    def _():
        halo_ref[n] = jnp.zeros(halo_ref.shape[1:], F32)

    row = lax.broadcasted_iota(jnp.int32, (ROW_CHUNK, tn), 0)
    halo = halo_ref[n]
    for c in range(tm // ROW_CHUNK):
        rows = slice(c * ROW_CHUNK, (c + 1) * ROW_CHUNK)
        h = jnp.dot(x_ref[rows, :], w_ref[...], preferred_element_type=F32)
        prev, halo = halo, h[ROW_CHUNK - 8:, :]
        acts = []
        for half, (cw_ref, cb_ref) in enumerate(((cwg_ref, cbg_ref), (cwv_ref, cbv_ref))):
            hh = h[:, half * tn:(half + 1) * tn]
            hl = prev[:, half * tn:(half + 1) * tn]
            h1 = jnp.where(row == 0, hl[7:8], pltpu.roll(hh, 1, 0))
            h2 = jnp.where(row == 0, hl[6:7], jnp.where(row == 1, hl[7:8], pltpu.roll(hh, 2, 0)))
            cw = cw_ref[...]
            acts.append(cb_ref[...] + cw[0:1] * h2 + cw[1:2] * h1 + cw[2:3] * hh)
        gate, val = acts
        o_ref[rows, :] = (gate * jax.nn.sigmoid(gate) * val).astype(o_ref.dtype)
    halo_ref[n] = halo


def up_conv_glu(xb, wub, conv_w, conv_b, w_down, layer, *, seq, tm, tn):
    m, k = xb.shape
    dff = wub.shape[1] // 2
    nb = dff // tn
    nm = m // tm
    assert seq % tm == 0 and dff % tn == 0
    kd, dd = w_down.shape[-2:]
    slab = kd // (nb * nm)
    assert slab * nb * nm == kd and slab % 16 == 0
    conv_b = conv_b.reshape(conv_b.shape[0], 1, 2 * dff)
    return pl.pallas_call(
        functools.partial(_up_kernel, seq=seq),
        grid=(nm, nb),
        in_specs=[
            pl.BlockSpec((tm, k), lambda i, n: (i, 0)),
            pl.BlockSpec((k, 2 * tn), lambda i, n: (0, n)),
            pl.BlockSpec((None, CONV_WIDTH, tn), lambda i, n: (layer, 0, n)),
            pl.BlockSpec((None, CONV_WIDTH, tn), lambda i, n: (layer, 0, nb + n)),
            pl.BlockSpec((None, 1, tn), lambda i, n: (layer, 0, n)),
            pl.BlockSpec((None, 1, tn), lambda i, n: (layer, 0, nb + n)),
            pl.BlockSpec((None, slab, dd), lambda i, n: (layer, i * nb + n, 0)),
        ],
        out_specs=[pl.BlockSpec((tm, tn), lambda i, n: (i, n)),
                   pl.BlockSpec((slab, dd), lambda i, n: (i * nb + n, 0))],
        out_shape=[jax.ShapeDtypeStruct((m, dff), BF16), jax.ShapeDtypeStruct((kd, dd), BF16)],
        scratch_shapes=[pltpu.VMEM((nb, 8, 2 * tn), F32)],
        compiler_params=_params(2),
        name="up_conv_glu",
    )(xb, wub, conv_w, conv_w, conv_b, conv_b, w_down)


class _Tiles:
    mm_rows = 1024
    qkv_cols = 768
    qkv_c_cols = 768
    gate_cols = 768
    branch_cols = 1024
    out_cols = 1024
    up_rows = 2048
    up_cols = 256
    down_rows = 512
    down_cols = 512
    attn_a_q = 2048
    attn_a_k = 512
    banded_tokens = 512
    strided_tokens = 4096
    ln_rows = 512
    cast_rows = 512


def _forward(x, w_in, diff_lambda, diff_norm_g, sink_logits, w_branch, w_o, ln1_g, ln1_b,
             w_up, conv_w, conv_b, w_down, ln2_g, ln2_b, tiles=_Tiles):
    batch, seq, d = x.shape
    depth = w_in.shape[0]
    tokens = batch * seq
    alpha = (2 * depth) ** 0.25

    slopes = jnp.exp2(-8.0 * (jnp.arange(N_ALIBI_HEADS, dtype=F32) + 1.0) / N_ALIBI_HEADS)
    slopes2 = slopes * LOG2E
    colscale = np.ones((1, QKV_WIDTH), np.float32)
    for off, wdt in ((OFF_AQ, A_WIDTH), (OFF_BQ, B_WIDTH), (OFF_CQ, C_WIDTH)):
        colscale[:, off:off + wdt] = SCALE * LOG2E
    colscale = jnp.asarray(colscale)

    x32 = x.reshape(tokens, d)
    resid = (x32,)
    xb = cast_bf16(x32, tiles.cast_rows)
    for l in range(depth):
        lam_init = 0.8 - 0.6 * math.exp(-0.3 * l)
        qkv, wbb, wob = in_projection(xb, w_in, l, 0, OFF_CQ, colscale=colscale[:, :OFF_CQ], out_dtype=BF16,
                                      tm=tiles.mm_rows, tn=tiles.qkv_cols, name="inproj_qkv",
                                      side=((w_branch, None), (w_o, None)))
        qkv_c = in_projection(xb, w_in, l, OFF_CQ, 3 * C_WIDTH, colscale=colscale[:, OFF_CQ:], out_dtype=F32,
                              tm=tiles.mm_rows, tn=tiles.qkv_c_cols, name="inproj_qkv_c")
        gates, wub = in_projection(xb, w_in, l, QKV_WIDTH, 3 * d, colscale=None, out_dtype=BF16,
                                   tm=tiles.mm_rows, tn=tiles.gate_cols, name="inproj_gates",
                                   side=((w_up, tiles.up_cols),))
        o_a = diff_attention(qkv, slopes2, diff_lambda[l], diff_norm_g[l], batch=batch, seq=seq,
                             lam_init=lam_init, tq=tiles.attn_a_q, tk=tiles.attn_a_k)
        (o_b,) = banded_attention(
            qkv, slopes2, sink_logits[l] * LOG2E, batch=batch, seq=seq, dil=1, tq=tiles.banded_tokens,
            hk=1, g=B_Q_HEADS // B_KV_HEADS, n_steps=B_KV_HEADS, q_off=OFF_BQ, k_off=OFF_BK, v_off=OFF_BV,
            max_dist=B_WINDOW - 1, slope0=0, out_width=B_WIDTH, out_dtype=BF16, want_lse=False)
        c_outs, c_lses = [], []
        gw = C_HEADS_PER_GROUP * HEAD_DIM
        for gi, (window, dil) in enumerate(C_GROUPS):
            hk = C_HEADS_PER_GROUP if dil == 1 else 1
            o, lse = banded_attention(
                qkv_c, slopes2, None, batch=batch, seq=seq, dil=dil,
                tq=tiles.banded_tokens if dil == 1 else max(tiles.strided_tokens // dil, BLOCK),
                hk=hk, g=1, n_steps=C_HEADS_PER_GROUP // hk,
                q_off=gi * gw, k_off=C_WIDTH + gi * gw, v_off=2 * C_WIDTH + gi * gw,
                max_dist=window // dil, slope0=B_Q_HEADS + gi * C_HEADS_PER_GROUP,
                out_width=gw, out_dtype=F32, want_lse=True)
            c_outs.append(o)
            c_lses.append(lse)
        o_c = merge_groups(c_outs, c_lses, tiles.ln_rows)
        y = branch_projection(o_a, o_b, o_c, gates, wbb, tm=tiles.mm_rows, tn=tiles.branch_cols)
        pre = residual_matmul_bf16(y, wob, resid, alpha=alpha, tm=tiles.mm_rows, tn=tiles.out_cols,
                                   name="out_projection")
        xb, resid = layer_norm(pre, ln1_g[l], ln1_b[l], tm=tiles.ln_rows, last=False)
        act, wdb = up_conv_glu(xb, wub, conv_w, conv_b, w_down, l, seq=seq, tm=tiles.up_rows, tn=tiles.up_cols)
        pre = residual_matmul_bf16(act, wdb, resid, alpha=alpha, tm=tiles.down_rows, tn=tiles.down_cols,
                                   name="down_projection")
        if l + 1 < depth:
            xb, resid = layer_norm(pre, ln2_g[l], ln2_b[l], tm=tiles.ln_rows, last=False)
    out = layer_norm(pre, ln2_g[depth - 1], ln2_b[depth - 1], tm=tiles.ln_rows, last=True)
    return out.reshape(batch, seq, d)


def kernel(x, w_in, diff_lambda, diff_norm_g, sink_logits, w_branch, w_o, ln1_g, ln1_b,
           w_up, conv_w, conv_b, w_down, ln2_g, ln2_b):
    return _forward(x, w_in, diff_lambda, diff_norm_g, sink_logits, w_branch, w_o, ln1_g, ln1_b,
                    w_up, conv_w, conv_b, w_down, ln2_g, ln2_b)
```
